```python
import jax, jax.numpy as jnp
from jax import lax
import numpy as np

D_MODEL = 2048
BATCH = 4
SEQ = 4096
DEPTH = 1

HEAD_DIM = 128
RET_HEADS = 8
GDN_HEADS = 8
RET_WIDTH = RET_HEADS * HEAD_DIM
GDN_WIDTH = GDN_HEADS * HEAD_DIM
MIX_WIDTH = RET_WIDTH + GDN_WIDTH
D_FF = 5632
CHUNK = 64
CONV_K = 4
ROPE_THETA = 10000.0
EPS = 1e-6
N_IN = 4 * RET_WIDTH + 4 * GDN_WIDTH + 2 * GDN_HEADS

kernel_name = "hybrid_retention_gdn_macaron"


def _rmsnorm(x, w):
    xf = x.astype(jnp.float32)
    y = xf * lax.rsqrt(jnp.mean(xf * xf, axis=-1, keepdims=True) + EPS)
    return (y * w.astype(jnp.float32)).astype(x.dtype)


def _swiglu(x, w_gate, w_up, w_down):
    return (jax.nn.silu(x @ w_gate) * (x @ w_up)) @ w_down


def _rotary(x, positions):
    half = HEAD_DIM // 2
    inv_freq = ROPE_THETA ** (-jnp.arange(half, dtype=jnp.float32) / half)
    ang = positions.astype(jnp.float32)[..., None] * inv_freq
    cos = jnp.cos(ang)[:, :, None, :]
    sin = jnp.sin(ang)[:, :, None, :]
    x1, x2 = x[..., :half], x[..., half:]
    return jnp.concatenate([x1 * cos - x2 * sin, x2 * cos + x1 * sin], axis=-1)


def _l2norm(x):
    return x * lax.rsqrt(jnp.sum(x * x, axis=-1, keepdims=True) + EPS)


def _to_chunks(x):
    b, t, h = x.shape[:3]
    x = x.reshape((b, t // CHUNK, CHUNK, h) + x.shape[3:])
    return jnp.moveaxis(x, 3, 1)


def _from_chunks(x):
    b, h, n, c, d = x.shape
    return jnp.moveaxis(x, 1, 3).reshape(b, n * c, h, d)


def _retention(q, k, v):
    idx = jnp.arange(CHUNK, dtype=jnp.float32)
    log_gamma = jnp.log(1.0 - jnp.exp2(-5.0 - jnp.arange(RET_HEADS, dtype=jnp.float32)))
    rel = idx[:, None] - idx[None, :]
    causal = rel >= 0
    decay = jnp.where(causal, jnp.exp(log_gamma[:, None, None] * jnp.where(causal, rel, 0.0)), 0.0)
    scores = jnp.einsum('bhncd,bhnsd->bhncs', q, k) * decay[None, :, None]
    o_intra = jnp.einsum('bhncs,bhnse->bhnce', scores, v)
    q_dec = q * jnp.exp(log_gamma[:, None] * (idx + 1.0))[None, :, None, :, None]
    k_dec = k * jnp.exp(log_gamma[:, None] * (CHUNK - 1.0 - idx))[None, :, None, :, None]
    kv = jnp.einsum('bhncd,bhnce->bhnde', k_dec, v)
    chunk_decay = jnp.exp(log_gamma * CHUNK)[None, :, None, None]

    def step(state, inp):
        qd, kv_n = inp
        out = jnp.einsum('bhcd,bhde->bhce', qd, state)
        return chunk_decay * state + kv_n, out

    b, h, n, c, d = q.shape
    init = jnp.zeros((b, h, d, v.shape[-1]), jnp.float32)
    _, o_inter = lax.scan(step, init, (jnp.moveaxis(q_dec, 2, 0), jnp.moveaxis(kv, 2, 0)))
    return o_intra + jnp.moveaxis(o_inter, 0, 2)


def _gated_delta(q, k, v, g, beta):
    idx = jnp.arange(CHUNK)
    causal = idx[:, None] >= idx[None, :]
    strict = idx[:, None] > idx[None, :]
    gc = jnp.cumsum(g, axis=-1)
    decay = jnp.exp(jnp.where(causal, gc[..., :, None] - gc[..., None, :], -jnp.inf))
    k_beta = k * beta[..., None]
    m = jnp.where(strict, jnp.einsum('bhncd,bhnsd->bhncs', k_beta, k) * decay, 0.0)
    a = m + jnp.eye(CHUNK, dtype=jnp.float32)
    rhs = jnp.concatenate([v * beta[..., None], k_beta * jnp.exp(gc)[..., None]], axis=-1)
    sol = lax.linalg.triangular_solve(a, rhs, left_side=True, lower=True, unit_diagonal=True)
    dv = v.shape[-1]
    u, w = sol[..., :dv], sol[..., dv:]
    attn = jnp.einsum('bhncd,bhnsd->bhncs', q, k) * decay
    q_dec = q * jnp.exp(gc)[..., None]
    k_dec = k * jnp.exp(gc[..., -1:] - gc)[..., None]
    chunk_decay = jnp.exp(gc[..., -1])

    def step(state, inp):
        u_n, w_n, qd, kd, at, cd = inp
        v_new = u_n - jnp.einsum('bhck,bhkv->bhcv', w_n, state)
        out = jnp.einsum('bhck,bhkv->bhcv', qd, state) + jnp.einsum('bhcs,bhsv->bhcv', at, v_new)
        state = state * cd[..., None, None] + jnp.einsum('bhck,bhcv->bhkv', kd, v_new)
        return state, out

    b, h, n, c, dk = q.shape
    init = jnp.zeros((b, h, dk, dv), jnp.float32)
    xs = tuple(jnp.moveaxis(z, 2, 0) for z in (u, w, q_dec, k_dec, attn, chunk_decay))
    _, out = lax.scan(step, init, xs)
    return jnp.moveaxis(out, 0, 2)


def _mixer(h, positions, w_in, conv_w, a_log, dt_bias, gdn_norm_w, w_out):
    b, t, _ = h.shape
    f32 = jnp.float32
    proj = h @ w_in
    o1 = 4 * RET_WIDTH
    o2 = o1 + 3 * GDN_WIDTH
    o3 = o2 + GDN_WIDTH
    o4 = o3 + GDN_HEADS
    ret, gqkv, gz, ga, gb = jnp.split(proj, [o1, o2, o3, o4], axis=-1)

    def heads(z, n):
        return z.reshape(b, t, n, HEAD_DIM).astype(f32)

    rq, rk, rv, rg = jnp.split(ret, 4, axis=-1)
    rq = _rotary(heads(rq, RET_HEADS), positions)
    rk = _rotary(heads(rk, RET_HEADS), positions) * HEAD_DIM ** -0.5
    rv = heads(rv, RET_HEADS)
    ro = _from_chunks(_retention(_to_chunks(rq), _to_chunks(rk), _to_chunks(rv)))
    mu = jnp.mean(ro, axis=-1, keepdims=True)
    var = jnp.mean(jnp.square(ro - mu), axis=-1, keepdims=True)
    ro = (ro - mu) * lax.rsqrt(var + EPS)
    ro = ro.reshape(b, t, RET_WIDTH) * jax.nn.silu(rg.astype(f32))

    gqkv = jax.nn.silu(lax.conv_general_dilated(
        gqkv, conv_w.astype(gqkv.dtype)[:, None, :], window_strides=(1,),
        padding=[(CONV_K - 1, 0)], dimension_numbers=('NWC', 'WIO', 'NWC'),
        feature_group_count=3 * GDN_WIDTH))
    gq, gk, gv = jnp.split(gqkv, 3, axis=-1)
    gq = _l2norm(heads(gq, GDN_HEADS)) * HEAD_DIM ** -0.5
    gk = _l2norm(heads(gk, GDN_HEADS))
    gv = heads(gv, GDN_HEADS)
    g = -jnp.exp(a_log.astype(f32)) * jax.nn.softplus(ga.astype(f32) + dt_bias.astype(f32))
    beta = jax.nn.sigmoid(gb.astype(f32))
    go = _from_chunks(_gated_delta(_to_chunks(gq), _to_chunks(gk), _to_chunks(gv),
                                   _to_chunks(g), _to_chunks(beta)))
    go = go * lax.rsqrt(jnp.mean(go * go, axis=-1, keepdims=True) + EPS) * gdn_norm_w.astype(f32)
    go = (go * jax.nn.silu(heads(gz, GDN_HEADS))).reshape(b, t, GDN_WIDTH)

    mixed = jnp.concatenate([ro, go], axis=-1).astype(h.dtype)
    return mixed @ w_out


def setup_inputs(seed: int = 0) -> dict:
    key = jax.random.key(seed)
    ks = jax.random.split(key, 20)
    f32 = jnp.float32

    def nrm(k, shape, fan_in):
        return jax.random.normal(k, shape, f32) * fan_in ** -0.5

    def gain(k, shape):
        return 1.0 + 0.02 * jax.random.normal(k, shape, f32)

    x = jax.random.normal(ks[0], (BATCH, SEQ, D_MODEL), f32)
    positions = jnp.broadcast_to(jnp.arange(SEQ, dtype=jnp.int32), (BATCH, SEQ))
    dt = jnp.exp(jax.random.uniform(ks[10], (DEPTH, GDN_HEADS), f32, np.log(1e-3), np.log(1e-1)))
    return {
        "x": x,
        "positions": positions,
        "norm_ffn1_w": gain(ks[1], (DEPTH, D_MODEL)),
        "ffn1_w_gate": nrm(ks[2], (DEPTH, D_MODEL, D_FF), D_MODEL),
        "ffn1_w_up": nrm(ks[3], (DEPTH, D_MODEL, D_FF), D_MODEL),
        "ffn1_w_down": nrm(ks[4], (DEPTH, D_FF, D_MODEL), D_FF),
        "norm_mix_w": gain(ks[5], (DEPTH, D_MODEL)),
        "w_in": nrm(ks[6], (DEPTH, D_MODEL, N_IN), D_MODEL),
        "conv_w": nrm(ks[7], (DEPTH, CONV_K, 3 * GDN_WIDTH), CONV_K),
        "gdn_a_log": jnp.log(jax.random.uniform(ks[8], (DEPTH, GDN_HEADS), f32, 1.0, 16.0)),
        "gdn_dt_bias": dt + jnp.log(-jnp.expm1(-dt)),
        "gdn_norm_w": gain(ks[9], (DEPTH, HEAD_DIM)),
        "w_out": nrm(ks[11], (DEPTH, MIX_WIDTH, D_MODEL), MIX_WIDTH),
        "norm_ffn2_w": gain(ks[12], (DEPTH, D_MODEL)),
        "ffn2_w_gate": nrm(ks[13], (DEPTH, D_MODEL, D_FF), D_MODEL),
        "ffn2_w_up": nrm(ks[14], (DEPTH, D_MODEL, D_FF), D_MODEL),
        "ffn2_w_down": nrm(ks[15], (DEPTH, D_FF, D_MODEL), D_FF),
        "norm_final_w": gain(ks[16], (D_MODEL,)),
    }


def reference(x, positions, norm_ffn1_w, ffn1_w_gate, ffn1_w_up, ffn1_w_down,
              norm_mix_w, w_in, conv_w, gdn_a_log, gdn_dt_bias, gdn_norm_w, w_out,
              norm_ffn2_w, ffn2_w_gate, ffn2_w_up, ffn2_w_down, norm_final_w):
    for l in range(DEPTH):
        x = x + 0.5 * _swiglu(_rmsnorm(x, norm_ffn1_w[l]), ffn1_w_gate[l], ffn1_w_up[l], ffn1_w_down[l])
        x = x + _mixer(_rmsnorm(x, norm_mix_w[l]), positions, w_in[l], conv_w[l],
                       gdn_a_log[l], gdn_dt_bias[l], gdn_norm_w[l], w_out[l])
        x = x + 0.5 * _swiglu(_rmsnorm(x, norm_ffn2_w[l]), ffn2_w_gate[l], ffn2_w_up[l], ffn2_w_down[l])
    return _rmsnorm(x, norm_final_w)
```

```python
import functools
import math

import jax
import jax.numpy as jnp
from jax import lax
from jax.experimental import pallas as pl
from jax.experimental.pallas import tpu as pltpu

F32 = jnp.float32
BF16 = jnp.bfloat16

D_MODEL = 2048
HEAD_DIM = 128
RET_HEADS = 8
GDN_HEADS = 8
RET_WIDTH = RET_HEADS * HEAD_DIM
GDN_WIDTH = GDN_HEADS * HEAD_DIM
D_FF = 5632
GDN_CHUNK = 64
CONV_K = 4
ROPE_THETA = 10000.0
EPS = 1e-6

V7X_LANES = 128
V7X_SUBLANES = 8
V7X_VMEM_BYTES = 64 * 2**20

FFN_TM, FFN_TF = 512, 512
INPROJ_TM, INPROJ_TN = 1024, 1024
OUTPROJ_TM = 512
RET_TB = 256
GDN_TB = 256
SMALL_COLS = V7X_LANES


def _vmem_limit(pipelined_bytes, scratch_bytes, temp_bytes):
    need = 2 * pipelined_bytes + scratch_bytes + temp_bytes
    return int(min(need, V7X_VMEM_BYTES * 7 // 8))


def _mm(a, b):
    return jnp.dot(a.astype(BF16), b.astype(BF16), preferred_element_type=F32)


def _mm_nt(a, b):
    return lax.dot_general(a.astype(BF16), b.astype(BF16), (((1,), (1,)), ((), ())),
                           preferred_element_type=F32)


def _mm_tn(a, b):
    return lax.dot_general(a.astype(BF16), b.astype(BF16), (((0,), (0,)), ((), ())),
                           preferred_element_type=F32)


def _rmsnorm(x, w):
    return x * lax.rsqrt(jnp.mean(x * x, axis=-1, keepdims=True) + EPS) * w


def _silu(x):
    return x * jax.nn.sigmoid(x)


def _ffn_kernel(x_ref, nw_ref, wg_ref, wu_ref, wd_ref, fw_ref, o_ref, h_ref, *, final_norm):
    j = pl.program_id(1)

    @pl.when(j == 0)
    def _():
        x = x_ref[...]
        h_ref[...] = _rmsnorm(x, nw_ref[...]).astype(BF16)
        o_ref[...] = x

    h = h_ref[...]
    g = jnp.dot(h, wg_ref[...], preferred_element_type=F32)
    u = jnp.dot(h, wu_ref[...], preferred_element_type=F32)
    act = (_silu(g) * u) * 0.5
    o_ref[...] += jnp.dot(act.astype(BF16), wd_ref[...], preferred_element_type=F32)

    if final_norm:
        @pl.when(j == pl.num_programs(1) - 1)
        def _():
            o_ref[...] = _rmsnorm(o_ref[...], fw_ref[...])


def _ffn(x, norm_w, wg, wu, wd, final_w, *, final_norm):
    m, d = x.shape
    f = wg.shape[1]
    tm, tf = FFN_TM, FFN_TF
    blocks = tm * d * 4 * 2 + 3 * d * tf * 2 + 2 * d * 4
    return pl.pallas_call(
        functools.partial(_ffn_kernel, final_norm=final_norm),
        grid=(m // tm, f // tf),
        in_specs=[
            pl.BlockSpec((tm, d), lambda i, j: (i, 0)),
            pl.BlockSpec((1, d), lambda i, j: (0, 0)),
            pl.BlockSpec((d, tf), lambda i, j: (0, j)),
            pl.BlockSpec((d, tf), lambda i, j: (0, j)),
            pl.BlockSpec((tf, d), lambda i, j: (j, 0)),
            pl.BlockSpec((1, d), lambda i, j: (0, 0)),
        ],
        out_specs=pl.BlockSpec((tm, d), lambda i, j: (i, 0)),
        out_shape=jax.ShapeDtypeStruct((m, d), F32),
        scratch_shapes=[pltpu.VMEM((tm, d), BF16)],
        compiler_params=pltpu.CompilerParams(
            dimension_semantics=("parallel", "arbitrary"),
            vmem_limit_bytes=_vmem_limit(blocks, tm * d * 2, 4 * tm * tf * 4 + tm * d * 4)),
        name="ffn_final" if final_norm else "ffn",
    )(x, norm_w, wg, wu, wd, final_w)


def _inproj_kernel(x_ref, nw_ref, w_ref, ws_ref, o_ref, os_ref, h_ref):
    j = pl.program_id(1)

    @pl.when(j == 0)
    def _():
        hb = _rmsnorm(x_ref[...], nw_ref[...]).astype(BF16)
        h_ref[...] = hb
        os_ref[...] = jnp.dot(hb, ws_ref[...], preferred_element_type=F32)

    o_ref[...] = jnp.dot(h_ref[...], w_ref[...], preferred_element_type=F32)


def _inproj(x, norm_w, w_main, w_small):
    m, d = x.shape
    n = w_main.shape[1]
    tm, tn = INPROJ_TM, INPROJ_TN
    blocks = tm * d * 4 + d * tn * 2 + d * SMALL_COLS * 2 + tm * tn * 4 + tm * SMALL_COLS * 4 + d * 4
    return pl.pallas_call(
        _inproj_kernel,
        grid=(m // tm, n // tn),
        in_specs=[
            pl.BlockSpec((tm, d), lambda i, j: (i, 0)),
            pl.BlockSpec((1, d), lambda i, j: (0, 0)),
            pl.BlockSpec((d, tn), lambda i, j: (0, j)),
            pl.BlockSpec((d, SMALL_COLS), lambda i, j: (0, 0)),
        ],
        out_specs=[
            pl.BlockSpec((tm, tn), lambda i, j: (i, j)),
            pl.BlockSpec((tm, SMALL_COLS), lambda i, j: (i, 0)),
        ],
        out_shape=[
            jax.ShapeDtypeStruct((m, n), F32),
            jax.ShapeDtypeStruct((m, SMALL_COLS), F32),
        ],
        scratch_shapes=[pltpu.VMEM((tm, d), BF16)],
        compiler_params=pltpu.CompilerParams(
            dimension_semantics=("parallel", "arbitrary"),
            vmem_limit_bytes=_vmem_limit(blocks, tm * d * 2, tm * d * 4 + tm * tn * 4)),
        name="inproj",
    )(x, norm_w, w_main, w_small)


def _retention_kernel(pos_ref, invf_ref, sgn_ref, q_ref, k_ref, v_ref, g_ref, o_ref, state_ref, *, tb):
    @pl.when(pl.program_id(1) == 0)
    def _():
        state_ref[...] = jnp.zeros_like(state_ref)

    pos = pos_ref[0].astype(F32)
    pos_rows = jnp.broadcast_to(pos, (V7X_LANES, tb)).T
    ang = pos_rows * invf_ref[...]
    cos2 = jnp.cos(ang)
    sin2 = jnp.sin(ang) * sgn_ref[...]

    ri = lax.broadcasted_iota(jnp.int32, (tb, tb), 0)
    ci = lax.broadcasted_iota(jnp.int32, (tb, tb), 1)
    causal = ri >= ci
    rel = jnp.where(causal, (ri - ci).astype(F32), 0.0)
    idx = lax.broadcasted_iota(jnp.int32, (tb, HEAD_DIM), 0).astype(F32)
    scale = HEAD_DIM ** -0.5

    for h in range(RET_HEADS):
        log_gamma = math.log(1.0 - 2.0 ** (-5 - h))
        sl = slice(h * HEAD_DIM, (h + 1) * HEAD_DIM)
        q = q_ref[:, sl]
        k = k_ref[:, sl]
        v = v_ref[:, sl]
        qr = q * cos2 + pltpu.roll(q, HEAD_DIM // 2, 1) * sin2
        kr = (k * cos2 + pltpu.roll(k, HEAD_DIM // 2, 1) * sin2) * scale
        decay = jnp.where(causal, jnp.exp(log_gamma * rel), 0.0)
        o = _mm(_mm_nt(qr, kr) * decay, v)
        state = state_ref[h]
        o = o + _mm(qr * jnp.exp(log_gamma * (idx + 1.0)), state)
        kv = _mm_tn(kr * jnp.exp(log_gamma * (tb - 1.0 - idx)), v)
        state_ref[h] = math.exp(log_gamma * tb) * state + kv
        mu = jnp.mean(o, axis=-1, keepdims=True)
        oc = o - mu
        var = jnp.mean(oc * oc, axis=-1, keepdims=True)
        o_ref[:, sl] = (oc * lax.rsqrt(var + EPS) * _silu(g_ref[:, sl])).astype(BF16)


def _retention(proj, pos3, invf2, sgn2, batch, seq):
    tb = RET_TB
    nt = seq // tb
    m = batch * seq
    col = lambda c: pl.BlockSpec((tb, RET_WIDTH), lambda b, t, c=c: (b * nt + t, c))
    blocks = 4 * tb * RET_WIDTH * 4 + tb * RET_WIDTH * 2 + tb * 4 + 2 * V7X_LANES * 4
    return pl.pallas_call(
        functools.partial(_retention_kernel, tb=tb),
        grid=(batch, nt),
        in_specs=[
            pl.BlockSpec((1, 1, tb), lambda b, t: (b * nt + t, 0, 0)),
            pl.BlockSpec((1, V7X_LANES), lambda b, t: (0, 0)),
            pl.BlockSpec((1, V7X_LANES), lambda b, t: (0, 0)),
            col(0), col(1), col(2), col(3),
        ],
        out_specs=pl.BlockSpec((tb, RET_WIDTH), lambda b, t: (b * nt + t, 0)),
        out_shape=jax.ShapeDtypeStruct((m, RET_WIDTH), BF16),
        scratch_shapes=[pltpu.VMEM((RET_HEADS, HEAD_DIM, HEAD_DIM), F32)],
        compiler_params=pltpu.CompilerParams(
            dimension_semantics=("arbitrary", "arbitrary"),
            vmem_limit_bytes=_vmem_limit(blocks, RET_HEADS * HEAD_DIM * HEAD_DIM * 4,
                                         16 * tb * tb * 4 + 8 * tb * RET_WIDTH * 4)),
        name="retention",
    )(pos3, invf2, sgn2, proj, proj, proj, proj)


def _softplus(x):
    return jnp.maximum(x, 0.0) + jnp.log1p(jnp.exp(-jnp.abs(x)))


def _unit_lower_inverse(m, eye, blk16, blk32):
    d = jnp.where(blk16, m, 0.0)
    d2 = _mm(d, d)
    d4 = _mm(d2, d2)
    d8 = _mm(d4, d4)
    p = _mm(eye - d, eye + d2)
    p = _mm(p, eye + d4)
    p = _mm(p, eye + d8)
    l1 = jnp.where(blk32 & jnp.logical_not(blk16), m, 0.0)
    p = p - _mm(p, _mm(l1, p))
    l2 = jnp.where(blk32, 0.0, m)
    p = p - _mm(p, _mm(l2, p))
    return p


def _gdn_kernel(q_ref, k_ref, v_ref, z_ref, s_ref, cw_ref, alog_ref, dtb_ref, nw_ref,
                o_ref, pad_ref, state_ref, *, tb):
    c_len = GDN_CHUNK
    nc = tb // c_len
    halo = V7X_SUBLANES

    @pl.when(pl.program_id(1) == 0)
    def _():
        state_ref[...] = jnp.zeros_like(state_ref)
        pad_ref[:, 0:halo, :] = jnp.zeros((3, halo, GDN_WIDTH), F32)

    conv = []
    for s, ref in enumerate((q_ref, k_ref, v_ref)):
        x = ref[...]
        pad_ref[s, halo:halo + tb, :] = x
        w = cw_ref[:, s * GDN_WIDTH:(s + 1) * GDN_WIDTH]
        acc = x * w[CONV_K - 1:CONV_K, :]
        for back in range(1, CONV_K):
            acc = acc + pad_ref[s, halo - back:halo - back + tb, :] * w[CONV_K - 1 - back:CONV_K - back, :]
        pad_ref[s, 0:halo, :] = pad_ref[s, tb:tb + halo, :]
        conv.append(_silu(acc))
    cq, ck, cv = conv

    small = s_ref[...]
    lane = lax.broadcasted_iota(jnp.int32, (tb, SMALL_COLS), 1)
    g_all = jnp.where(lane < GDN_HEADS, -jnp.exp(alog_ref[...]) * _softplus(small + dtb_ref[...]), 0.0)
    beta_all = jax.nn.sigmoid(small)

    ri = lax.broadcasted_iota(jnp.int32, (tb, tb), 0)
    ci = lax.broadcasted_iota(jnp.int32, (tb, tb), 1)
    chunk_tri = ((ri >= ci) & ((ri >> 6) == (ci >> 6))).astype(F32)
    gc = jnp.dot(chunk_tri, g_all, precision=lax.Precision.HIGHEST, preferred_element_type=F32)
    gct = gc.T

    r64 = lax.broadcasted_iota(jnp.int32, (c_len, c_len), 0)
    c64 = lax.broadcasted_iota(jnp.int32, (c_len, c_len), 1)
    causal = r64 >= c64
    strict = r64 > c64
    eye = (r64 == c64).astype(F32)
    blk16 = (r64 >> 4) == (c64 >> 4)
    blk32 = (r64 >> 5) == (c64 >> 5)
    scale = HEAD_DIM ** -0.5

    for h in range(GDN_HEADS):
        sl = slice(h * HEAD_DIM, (h + 1) * HEAD_DIM)
        qh = cq[:, sl]
        kh = ck[:, sl]
        vh = cv[:, sl]
        qn = qh * lax.rsqrt(jnp.sum(qh * qh, axis=-1, keepdims=True) + EPS) * scale
        kn = kh * lax.rsqrt(jnp.sum(kh * kh, axis=-1, keepdims=True) + EPS)
        state = state_ref[h]
        outs = []
        for c in range(nc):
            rows = slice(c * c_len, (c + 1) * c_len)
            q_c, k_c, v_c = qn[rows], kn[rows], vh[rows]
            g_col = gc[rows, h:h + 1]
            g_row = gct[h:h + 1, rows]
            b_col = beta_all[rows, GDN_HEADS + h:GDN_HEADS + h + 1]
            decay = jnp.exp(jnp.where(causal, g_col - g_row, -jnp.inf))
            kb = k_c * b_col
            m = jnp.where(strict, _mm_nt(kb, k_c) * decay, 0.0)
            t_inv = _unit_lower_inverse(m, eye, blk16, blk32)
            e_g = jnp.exp(g_col)
            rhs = jnp.concatenate([v_c * b_col, kb * e_g], axis=1)
            sol = rhs + _mm(t_inv - eye, rhs)
            u, w = sol[:, :HEAD_DIM], sol[:, HEAD_DIM:]
            attn = _mm_nt(q_c, k_c) * decay
            g_last = g_col[c_len - 1:c_len, :]
            v_new = u - _mm(w, state)
            outs.append(_mm(q_c * e_g, state) + _mm(attn, v_new))
            state = state * jnp.exp(g_last) + _mm_tn(k_c * jnp.exp(g_last - g_col), v_new)
        state_ref[h] = state
        o = jnp.concatenate(outs, axis=0)
        o = o * lax.rsqrt(jnp.mean(o * o, axis=-1, keepdims=True) + EPS) * nw_ref[...]
        o_ref[:, sl] = (o * _silu(z_ref[:, sl])).astype(BF16)


def _gdn(proj, small, conv_w, alog_row, dtb_row, norm_w, batch, seq):
    tb = GDN_TB
    nt = seq // tb
    m = batch * seq
    first = RET_WIDTH * 4 // GDN_WIDTH
    col = lambda c: pl.BlockSpec((tb, GDN_WIDTH), lambda b, t, c=c: (b * nt + t, first + c))
    row = lambda n: pl.BlockSpec((1, n), lambda b, t: (0, 0))
    blocks = (4 * tb * GDN_WIDTH * 4 + tb * SMALL_COLS * 4 + CONV_K * 3 * GDN_WIDTH * 4
              + tb * GDN_WIDTH * 2 + 3 * V7X_LANES * 4)
    scratch = 3 * (tb + V7X_SUBLANES) * GDN_WIDTH * 4 + GDN_HEADS * HEAD_DIM * HEAD_DIM * 4
    return pl.pallas_call(
        functools.partial(_gdn_kernel, tb=tb),
        grid=(batch, nt),
        in_specs=[
            col(0), col(1), col(2), col(3),
            pl.BlockSpec((tb, SMALL_COLS), lambda b, t: (b * nt + t, 0)),
            pl.BlockSpec((CONV_K, 3 * GDN_WIDTH), lambda b, t: (0, 0)),
            row(SMALL_COLS), row(SMALL_COLS), row(HEAD_DIM),
        ],
        out_specs=pl.BlockSpec((tb, GDN_WIDTH), lambda b, t: (b * nt + t, 0)),
        out_shape=jax.ShapeDtypeStruct((m, GDN_WIDTH), BF16),
        scratch_shapes=[
            pltpu.VMEM((3, tb + V7X_SUBLANES, GDN_WIDTH), F32),
            pltpu.VMEM((GDN_HEADS, HEAD_DIM, HEAD_DIM), F32),
        ],
        compiler_params=pltpu.CompilerParams(
            dimension_semantics=("arbitrary", "arbitrary"),
            vmem_limit_bytes=_vmem_limit(blocks, scratch, 12 * tb * GDN_WIDTH * 4)),
        name="gdn",
    )(proj, proj, proj, proj, small, conv_w, alog_row, dtb_row, norm_w)


def _outproj_kernel(x_ref, ro_ref, go_ref, wr_ref, wg_ref, o_ref):
    o_ref[...] = (x_ref[...]
                  + jnp.dot(ro_ref[...], wr_ref[...], preferred_element_type=F32)
                  + jnp.dot(go_ref[...], wg_ref[...], preferred_element_type=F32))


def _outproj(x, ro, go, w_ret, w_gdn):
    m, d = x.shape
    tm = OUTPROJ_TM
    blocks = 2 * tm * d * 4 + tm * (RET_WIDTH + GDN_WIDTH) * 2 + (RET_WIDTH + GDN_WIDTH) * d * 2
    return pl.pallas_call(
        _outproj_kernel,
        grid=(m // tm,),
        in_specs=[
            pl.BlockSpec((tm, d), lambda i: (i, 0)),
            pl.BlockSpec((tm, RET_WIDTH), lambda i: (i, 0)),
            pl.BlockSpec((tm, GDN_WIDTH), lambda i: (i, 0)),
            pl.BlockSpec((RET_WIDTH, d), lambda i: (0, 0)),
            pl.BlockSpec((GDN_WIDTH, d), lambda i: (0, 0)),
        ],
        out_specs=pl.BlockSpec((tm, d), lambda i: (i, 0)),
        out_shape=jax.ShapeDtypeStruct((m, d), F32),
        compiler_params=pltpu.CompilerParams(
            dimension_semantics=("parallel",),
            vmem_limit_bytes=_vmem_limit(blocks, 0, 2 * tm * d * 4)),
        name="outproj",
    )(x, ro, go, w_ret, w_gdn)


def kernel(x, positions, norm_ffn1_w, ffn1_w_gate, ffn1_w_up, ffn1_w_down, norm_mix_w, w_in, conv_w,
           gdn_a_log, gdn_dt_bias, gdn_norm_w, w_out, norm_ffn2_w, ffn2_w_gate, ffn2_w_up, ffn2_w_down,
           norm_final_w):
    batch, seq, d = x.shape
    depth = norm_ffn1_w.shape[0]
    m = batch * seq
    n_main = 4 * RET_WIDTH + 4 * GDN_WIDTH
    row = lambda v: v.reshape(1, -1).astype(F32)

    half = HEAD_DIM // 2
    inv_freq = ROPE_THETA ** (-jnp.arange(half, dtype=F32) / half)
    invf2 = jnp.concatenate([inv_freq, inv_freq]).reshape(1, HEAD_DIM)
    sgn2 = jnp.concatenate([-jnp.ones((half,), F32), jnp.ones((half,), F32)]).reshape(1, HEAD_DIM)
    pos3 = positions.reshape(m // RET_TB, 1, RET_TB)
    pad_heads = lambda v: jnp.pad(v.astype(F32), (0, SMALL_COLS - GDN_HEADS)).reshape(1, SMALL_COLS)

    xf = x.reshape(m, d)
    for l in range(depth):
        last = l == depth - 1
        xf = _ffn(xf, row(norm_ffn1_w[l]), ffn1_w_gate[l].astype(BF16), ffn1_w_up[l].astype(BF16),
                  ffn1_w_down[l].astype(BF16), row(norm_final_w), final_norm=False)
        w_main = w_in[l][:, :n_main].astype(BF16)
        w_small = jnp.pad(w_in[l][:, n_main:], ((0, 0), (0, SMALL_COLS - 2 * GDN_HEADS))).astype(BF16)
        proj, small = _inproj(xf, row(norm_mix_w[l]), w_main, w_small)
        ro = _retention(proj, pos3, invf2, sgn2, batch, seq)
        go = _gdn(proj, small, conv_w[l].astype(F32), pad_heads(gdn_a_log[l]), pad_heads(gdn_dt_bias[l]),
                  row(gdn_norm_w[l]), batch, seq)
        w_o = w_out[l].astype(BF16)
        xf = _outproj(xf, ro, go, w_o[:RET_WIDTH], w_o[RET_WIDTH:])
        xf = _ffn(xf, row(norm_ffn2_w[l]), ffn2_w_gate[l].astype(BF16), ffn2_w_up[l].astype(BF16),
                  ffn2_w_down[l].astype(BF16), row(norm_final_w), final_norm=last)
    return xf.reshape(batch, seq, d)
```

```python
import functools
import math

import jax
import jax.numpy as jnp
from jax import lax
from jax.experimental import pallas as pl
from jax.experimental.pallas import tpu as pltpu

F32 = jnp.float32
BF16 = jnp.bfloat16

D_MODEL = 2048
HEAD_DIM = 128
RET_HEADS = 8
GDN_HEADS = 8
RET_WIDTH = RET_HEADS * HEAD_DIM
GDN_WIDTH = GDN_HEADS * HEAD_DIM
D_FF = 5632
GDN_CHUNK = 64
CONV_K = 4
ROPE_THETA = 10000.0
EPS = 1e-6

V7X_LANES = 128
V7X_SUBLANES = 8
V7X_VMEM_BYTES = 64 * 2**20

FFN_TM, FFN_TF = 1024, 512
INPROJ_TM, INPROJ_TN = 1024, 1024
OUTPROJ_TM = 512
RET_TB = 256
GDN_TB = 256
SMALL_COLS = V7X_LANES


def _vmem_limit(pipelined_bytes, scratch_bytes, temp_bytes):
    need = 2 * pipelined_bytes + scratch_bytes + temp_bytes
    return int(min(need, V7X_VMEM_BYTES * 7 // 8))


def _mm(a, b):
    return jnp.dot(a.astype(BF16), b.astype(BF16), preferred_element_type=F32)


def _mm_nt(a, b):
    return lax.dot_general(a.astype(BF16), b.astype(BF16), (((1,), (1,)), ((), ())),
                           preferred_element_type=F32)


def _mm_tn(a, b):
    return lax.dot_general(a.astype(BF16), b.astype(BF16), (((0,), (0,)), ((), ())),
                           preferred_element_type=F32)


def _rmsnorm(x, w):
    return x * lax.rsqrt(jnp.mean(x * x, axis=-1, keepdims=True) + EPS) * w


def _silu(x):
    return x * jax.nn.sigmoid(x)


def _ffn_kernel(x_ref, nw_ref, wg_ref, wu_ref, wd_ref, fw_ref, o_ref, h_ref, *, final_norm):
    j = pl.program_id(1)

    @pl.when(j == 0)
    def _():
        x = x_ref[...]
        h_ref[...] = _rmsnorm(x, nw_ref[...]).astype(BF16)
        o_ref[...] = x

    h = h_ref[...]
    g = jnp.dot(h, wg_ref[...], preferred_element_type=F32)
    u = jnp.dot(h, wu_ref[...], preferred_element_type=F32)
    act = (_silu(g) * u) * 0.5
    o_ref[...] += jnp.dot(act.astype(BF16), wd_ref[...], preferred_element_type=F32)

    if final_norm:
        @pl.when(j == pl.num_programs(1) - 1)
        def _():
            o_ref[...] = _rmsnorm(o_ref[...], fw_ref[...])


def _ffn(x, norm_w, wg, wu, wd, final_w, *, final_norm):
    m, d = x.shape
    f = wg.shape[1]
    tm, tf = FFN_TM, FFN_TF
    blocks = 2 * tm * d * 4 + 3 * d * tf * 2 + 2 * d * 4
    return pl.pallas_call(
        functools.partial(_ffn_kernel, final_norm=final_norm),
        grid=(m // tm, f // tf),
        in_specs=[
            pl.BlockSpec((tm, d), lambda i, j: (i, 0)),
            pl.BlockSpec((1, d), lambda i, j: (0, 0)),
            pl.BlockSpec((d, tf), lambda i, j: (0, j)),
            pl.BlockSpec((d, tf), lambda i, j: (0, j)),
            pl.BlockSpec((tf, d), lambda i, j: (j, 0)),
            pl.BlockSpec((1, d), lambda i, j: (0, 0)),
        ],
        out_specs=pl.BlockSpec((tm, d), lambda i, j: (i, 0)),
        out_shape=jax.ShapeDtypeStruct((m, d), F32),
        scratch_shapes=[pltpu.VMEM((tm, d), BF16)],
        compiler_params=pltpu.CompilerParams(
            dimension_semantics=("parallel", "arbitrary"),
            vmem_limit_bytes=_vmem_limit(blocks, tm * d * 2, 4 * tm * tf * 4)),
        name="ffn_final" if final_norm else "ffn",
    )(x, norm_w, wg, wu, wd, final_w)


def _inproj_kernel(x_ref, nw_ref, w_ref, ws_ref, o_ref, os_ref, h_ref):
    j = pl.program_id(1)

    @pl.when(j == 0)
    def _():
        hb = _rmsnorm(x_ref[...], nw_ref[...]).astype(BF16)
        h_ref[...] = hb
        os_ref[...] = jnp.dot(hb, ws_ref[...], preferred_element_type=F32)

    o_ref[...] = jnp.dot(h_ref[...], w_ref[...], preferred_element_type=F32)


def _inproj(x, norm_w, w_main, w_small):
    m, d = x.shape
    n = w_main.shape[1]
    tm, tn = INPROJ_TM, INPROJ_TN
    blocks = tm * d * 4 + d * tn * 2 + d * SMALL_COLS * 2 + tm * tn * 4 + tm * SMALL_COLS * 4 + d * 4
    return pl.pallas_call(
        _inproj_kernel,
        grid=(m // tm, n // tn),
        in_specs=[
            pl.BlockSpec((tm, d), lambda i, j: (i, 0)),
            pl.BlockSpec((1, d), lambda i, j: (0, 0)),
            pl.BlockSpec((d, tn), lambda i, j: (0, j)),
            pl.BlockSpec((d, SMALL_COLS), lambda i, j: (0, 0)),
        ],
        out_specs=[
            pl.BlockSpec((tm, tn), lambda i, j: (i, j)),
            pl.BlockSpec((tm, SMALL_COLS), lambda i, j: (i, 0)),
        ],
        out_shape=[
            jax.ShapeDtypeStruct((m, n), F32),
            jax.ShapeDtypeStruct((m, SMALL_COLS), F32),
        ],
        scratch_shapes=[pltpu.VMEM((tm, d), BF16)],
        compiler_params=pltpu.CompilerParams(
            dimension_semantics=("parallel", "arbitrary"),
            vmem_limit_bytes=_vmem_limit(blocks, tm * d * 2, tm * d * 4 + tm * tn * 4)),
        name="inproj",
    )(x, norm_w, w_main, w_small)


def _retention_kernel(pos_ref, invf_ref, sgn_ref, q_ref, k_ref, v_ref, g_ref, o_ref, state_ref, *, tb):
    @pl.when(pl.program_id(1) == 0)
    def _():
        state_ref[...] = jnp.zeros_like(state_ref)

    pos = pos_ref[0].astype(F32)
    pos_rows = jnp.broadcast_to(pos, (V7X_LANES, tb)).T
    ang = pos_rows * invf_ref[...]
    cos2 = jnp.cos(ang)
    sin2 = jnp.sin(ang) * sgn_ref[...]

    ri = lax.broadcasted_iota(jnp.int32, (tb, tb), 0)
    ci = lax.broadcasted_iota(jnp.int32, (tb, tb), 1)
    causal = ri >= ci
    rel = jnp.where(causal, (ri - ci).astype(F32), 0.0)
    idx = lax.broadcasted_iota(jnp.int32, (tb, HEAD_DIM), 0).astype(F32)
    scale = HEAD_DIM ** -0.5

    heads = range(RET_HEADS)
    lanes = lambda h: slice(h * HEAD_DIM, (h + 1) * HEAD_DIM)
    log_gamma = [math.log(1.0 - 2.0 ** (-5 - h)) for h in heads]
    rot = lambda x: x * cos2 + pltpu.roll(x, HEAD_DIM // 2, 1) * sin2
    qr = [rot(q_ref[:, lanes(h)]) for h in heads]
    kr = [rot(k_ref[:, lanes(h)]) * scale for h in heads]
    v = [v_ref[:, lanes(h)] for h in heads]
    state = [state_ref[h] for h in heads]
    scores = [_mm_nt(qr[h], kr[h]) for h in heads]
    inter = [_mm(qr[h] * jnp.exp(log_gamma[h] * (idx + 1.0)), state[h]) for h in heads]
    kv = [_mm_tn(kr[h] * jnp.exp(log_gamma[h] * (tb - 1.0 - idx)), v[h]) for h in heads]
    intra = [_mm(scores[h] * jnp.where(causal, jnp.exp(log_gamma[h] * rel), 0.0), v[h]) for h in heads]
    for h in heads:
        state_ref[h] = math.exp(log_gamma[h] * tb) * state[h] + kv[h]
        o = intra[h] + inter[h]
        mu = jnp.mean(o, axis=-1, keepdims=True)
        oc = o - mu
        var = jnp.mean(oc * oc, axis=-1, keepdims=True)
        o_ref[:, lanes(h)] = (oc * lax.rsqrt(var + EPS) * _silu(g_ref[:, lanes(h)])).astype(BF16)


def _retention(proj, pos3, invf2, sgn2, batch, seq):
    tb = RET_TB
    nt = seq // tb
    m = batch * seq
    col = lambda c: pl.BlockSpec((tb, RET_WIDTH), lambda b, t, c=c: (b * nt + t, c))
    blocks = 4 * tb * RET_WIDTH * 4 + tb * RET_WIDTH * 2 + tb * 4 + 2 * V7X_LANES * 4
    return pl.pallas_call(
        functools.partial(_retention_kernel, tb=tb),
        grid=(batch, nt),
        in_specs=[
            pl.BlockSpec((1, 1, tb), lambda b, t: (b * nt + t, 0, 0)),
            pl.BlockSpec((1, V7X_LANES), lambda b, t: (0, 0)),
            pl.BlockSpec((1, V7X_LANES), lambda b, t: (0, 0)),
            col(0), col(1), col(2), col(3),
        ],
        out_specs=pl.BlockSpec((tb, RET_WIDTH), lambda b, t: (b * nt + t, 0)),
        out_shape=jax.ShapeDtypeStruct((m, RET_WIDTH), BF16),
        scratch_shapes=[pltpu.VMEM((RET_HEADS, HEAD_DIM, HEAD_DIM), F32)],
        compiler_params=pltpu.CompilerParams(
            dimension_semantics=("arbitrary", "arbitrary"),
            vmem_limit_bytes=_vmem_limit(blocks, RET_HEADS * HEAD_DIM * HEAD_DIM * 4,
                                         16 * tb * tb * 4 + 8 * tb * RET_WIDTH * 4)),
        name="retention",
    )(pos3, invf2, sgn2, proj, proj, proj, proj)


def _softplus(x):
    return jnp.maximum(x, 0.0) + jnp.log1p(jnp.exp(-jnp.abs(x)))


def _unit_lower_inverses(ms, eye, blk16, blk32):
    d = [jnp.where(blk16, m, 0.0) for m in ms]
    d2 = [_mm(a, a) for a in d]
    d4 = [_mm(a, a) for a in d2]
    d8 = [_mm(a, a) for a in d4]
    p = [_mm(eye - a, eye + b) for a, b in zip(d, d2)]
    p = [_mm(a, eye + b) for a, b in zip(p, d4)]
    p = [_mm(a, eye + b) for a, b in zip(p, d8)]
    off16 = blk32 & jnp.logical_not(blk16)
    lp = [_mm(jnp.where(off16, m, 0.0), a) for m, a in zip(ms, p)]
    p = [a - _mm(a, b) for a, b in zip(p, lp)]
    lp = [_mm(jnp.where(blk32, 0.0, m), a) for m, a in zip(ms, p)]
    p = [a - _mm(a, b) for a, b in zip(p, lp)]
    return p


def _gdn_kernel(q_ref, k_ref, v_ref, z_ref, s_ref, cw_ref, alog_ref, dtb_ref, nw_ref,
                o_ref, pad_ref, state_ref, *, tb):
    c_len = GDN_CHUNK
    nc = tb // c_len
    halo = V7X_SUBLANES

    @pl.when(pl.program_id(1) == 0)
    def _():
        state_ref[...] = jnp.zeros_like(state_ref)
        pad_ref[:, 0:halo, :] = jnp.zeros((3, halo, GDN_WIDTH), F32)

    conv = []
    for s, ref in enumerate((q_ref, k_ref, v_ref)):
        x = ref[...]
        pad_ref[s, halo:halo + tb, :] = x
        w = cw_ref[:, s * GDN_WIDTH:(s + 1) * GDN_WIDTH]
        acc = x * w[CONV_K - 1:CONV_K, :]
        for back in range(1, CONV_K):
            acc = acc + pad_ref[s, halo - back:halo - back + tb, :] * w[CONV_K - 1 - back:CONV_K - back, :]
        pad_ref[s, 0:halo, :] = pad_ref[s, tb:tb + halo, :]
        conv.append(_silu(acc))
    cq, ck, cv = conv

    small = s_ref[...]
    lane = lax.broadcasted_iota(jnp.int32, (tb, SMALL_COLS), 1)
    g_all = jnp.where(lane < GDN_HEADS, -jnp.exp(alog_ref[...]) * _softplus(small + dtb_ref[...]), 0.0)
    beta_all = jax.nn.sigmoid(small)

    ri = lax.broadcasted_iota(jnp.int32, (tb, tb), 0)
    ci = lax.broadcasted_iota(jnp.int32, (tb, tb), 1)
    chunk_tri = ((ri >= ci) & ((ri >> 6) == (ci >> 6))).astype(F32)
    gc = jnp.dot(chunk_tri, g_all, precision=lax.Precision.HIGHEST, preferred_element_type=F32)
    gct = gc.T

    r64 = lax.broadcasted_iota(jnp.int32, (c_len, c_len), 0)
    c64 = lax.broadcasted_iota(jnp.int32, (c_len, c_len), 1)
    causal = r64 >= c64
    strict = r64 > c64
    eye = (r64 == c64).astype(F32)
    blk16 = (r64 >> 4) == (c64 >> 4)
    blk32 = (r64 >> 5) == (c64 >> 5)
    scale = HEAD_DIM ** -0.5

    heads = range(GDN_HEADS)
    chains = [(h, c) for h in heads for c in range(nc)]
    rows = lambda c: slice(c * c_len, (c + 1) * c_len)
    lanes = lambda h: slice(h * HEAD_DIM, (h + 1) * HEAD_DIM)

    qn, kn, kb, vb, kbe, qe, g_b = [], [], [], [], [], [], []
    for h in heads:
        qh, kh, vh = cq[:, lanes(h)], ck[:, lanes(h)], cv[:, lanes(h)]
        g_h = jnp.broadcast_to(gc[:, h:h + 1], (tb, HEAD_DIM))
        b_h = jnp.broadcast_to(beta_all[:, GDN_HEADS + h:GDN_HEADS + h + 1], (tb, HEAD_DIM))
        e_h = jnp.exp(g_h)
        q_h = qh * lax.rsqrt(jnp.sum(qh * qh, axis=-1, keepdims=True) + EPS) * scale
        k_h = kh * lax.rsqrt(jnp.sum(kh * kh, axis=-1, keepdims=True) + EPS)
        kb_h = k_h * b_h
        qn.append(q_h), kn.append(k_h), kb.append(kb_h), vb.append(vh * b_h)
        kbe.append(kb_h * e_h), qe.append(q_h * e_h), g_b.append(g_h)

    decay = [jnp.exp(jnp.where(causal, g_b[h][rows(c), :c_len] - gct[h:h + 1, rows(c)], -jnp.inf))
             for h, c in chains]
    kq = [_mm_nt(jnp.concatenate([kb[h][rows(c)], qn[h][rows(c)]], axis=0), kn[h][rows(c)])
          for h, c in chains]
    ms = [jnp.where(strict, a[:c_len] * dc, 0.0) for a, dc in zip(kq, decay)]
    attn = [a[c_len:] * dc for a, dc in zip(kq, decay)]
    t_inv = _unit_lower_inverses(ms, eye, blk16, blk32)
    rhs = [jnp.concatenate([vb[h][rows(c)], kbe[h][rows(c)]], axis=1) for h, c in chains]
    sol = [r + _mm(t - eye, r) for t, r in zip(t_inv, rhs)]

    state = [state_ref[h] for h in heads]
    outs = [[] for _ in heads]
    for c in range(nc):
        last = (c + 1) * c_len - 1
        g_last = [g_b[h][last:last + 1, :] for h in heads]
        k_dec = [kn[h][rows(c)] * jnp.exp(g_last[h] - g_b[h][rows(c)]) for h in heads]
        w_s = [_mm(sol[h * nc + c][:, HEAD_DIM:], state[h]) for h in heads]
        q_s = [_mm(qe[h][rows(c)], state[h]) for h in heads]
        v_new = [sol[h * nc + c][:, :HEAD_DIM] - w_s[h] for h in heads]
        a_v = [_mm(attn[h * nc + c], v_new[h]) for h in heads]
        k_v = [_mm_tn(k_dec[h], v_new[h]) for h in heads]
        state = [state[h] * jnp.exp(g_last[h]) + k_v[h] for h in heads]
        for h in heads:
            outs[h].append(q_s[h] + a_v[h])

    for h in heads:
        state_ref[h] = state[h]
        o = jnp.concatenate(outs[h], axis=0)
        o = o * lax.rsqrt(jnp.mean(o * o, axis=-1, keepdims=True) + EPS) * nw_ref[...]
        o_ref[:, lanes(h)] = (o * _silu(z_ref[:, lanes(h)])).astype(BF16)


def _gdn(proj, small, conv_w, alog_row, dtb_row, norm_w, batch, seq):
    tb = GDN_TB
    nt = seq // tb
    m = batch * seq
    first = RET_WIDTH * 4 // GDN_WIDTH
    col = lambda c: pl.BlockSpec((tb, GDN_WIDTH), lambda b, t, c=c: (b * nt + t, first + c))
    row = lambda n: pl.BlockSpec((1, n), lambda b, t: (0, 0))
    blocks = (4 * tb * GDN_WIDTH * 4 + tb * SMALL_COLS * 4 + CONV_K * 3 * GDN_WIDTH * 4
              + tb * GDN_WIDTH * 2 + 3 * V7X_LANES * 4)
    scratch = 3 * (tb + V7X_SUBLANES) * GDN_WIDTH * 4 + GDN_HEADS * HEAD_DIM * HEAD_DIM * 4
    return pl.pallas_call(
        functools.partial(_gdn_kernel, tb=tb),
        grid=(batch, nt),
        in_specs=[
            col(0), col(1), col(2), col(3),
            pl.BlockSpec((tb, SMALL_COLS), lambda b, t: (b * nt + t, 0)),
            pl.BlockSpec((CONV_K, 3 * GDN_WIDTH), lambda b, t: (0, 0)),
            row(SMALL_COLS), row(SMALL_COLS), row(HEAD_DIM),
        ],
        out_specs=pl.BlockSpec((tb, GDN_WIDTH), lambda b, t: (b * nt + t, 0)),
        out_shape=jax.ShapeDtypeStruct((m, GDN_WIDTH), BF16),
        scratch_shapes=[
            pltpu.VMEM((3, tb + V7X_SUBLANES, GDN_WIDTH), F32),
            pltpu.VMEM((GDN_HEADS, HEAD_DIM, HEAD_DIM), F32),
        ],
        compiler_params=pltpu.CompilerParams(
            dimension_semantics=("arbitrary", "arbitrary"),
            vmem_limit_bytes=_vmem_limit(blocks, scratch, 12 * tb * GDN_WIDTH * 4)),
        name="gdn",
    )(proj, proj, proj, proj, small, conv_w, alog_row, dtb_row, norm_w)


def _outproj_kernel(x_ref, ro_ref, go_ref, wr_ref, wg_ref, o_ref):
    o_ref[...] = (x_ref[...]
                  + jnp.dot(ro_ref[...], wr_ref[...], preferred_element_type=F32)
                  + jnp.dot(go_ref[...], wg_ref[...], preferred_element_type=F32))


def _outproj(x, ro, go, w_ret, w_gdn):
    m, d = x.shape
    tm = OUTPROJ_TM
    blocks = 2 * tm * d * 4 + tm * (RET_WIDTH + GDN_WIDTH) * 2 + (RET_WIDTH + GDN_WIDTH) * d * 2
    return pl.pallas_call(
        _outproj_kernel,
        grid=(m // tm,),
        in_specs=[
            pl.BlockSpec((tm, d), lambda i: (i, 0)),
            pl.BlockSpec((tm, RET_WIDTH), lambda i: (i, 0)),
            pl.BlockSpec((tm, GDN_WIDTH), lambda i: (i, 0)),
            pl.BlockSpec((RET_WIDTH, d), lambda i: (0, 0)),
            pl.BlockSpec((GDN_WIDTH, d), lambda i: (0, 0)),
        ],
        out_specs=pl.BlockSpec((tm, d), lambda i: (i, 0)),
        out_shape=jax.ShapeDtypeStruct((m, d), F32),
        compiler_params=pltpu.CompilerParams(
            dimension_semantics=("parallel",),
            vmem_limit_bytes=_vmem_limit(blocks, 0, 2 * tm * d * 4)),
        name="outproj",
    )(x, ro, go, w_ret, w_gdn)


def kernel(x, positions, norm_ffn1_w, ffn1_w_gate, ffn1_w_up, ffn1_w_down, norm_mix_w, w_in, conv_w,
           gdn_a_log, gdn_dt_bias, gdn_norm_w, w_out, norm_ffn2_w, ffn2_w_gate, ffn2_w_up, ffn2_w_down,
           norm_final_w):
    batch, seq, d = x.shape
    depth = norm_ffn1_w.shape[0]
    m = batch * seq
    n_main = 4 * RET_WIDTH + 4 * GDN_WIDTH
    row = lambda v: v.reshape(1, -1).astype(F32)

    half = HEAD_DIM // 2
    inv_freq = ROPE_THETA ** (-jnp.arange(half, dtype=F32) / half)
    invf2 = jnp.concatenate([inv_freq, inv_freq]).reshape(1, HEAD_DIM)
    sgn2 = jnp.concatenate([-jnp.ones((half,), F32), jnp.ones((half,), F32)]).reshape(1, HEAD_DIM)
    pos3 = positions.reshape(m // RET_TB, 1, RET_TB)
    pad_heads = lambda v: jnp.pad(v.astype(F32), (0, SMALL_COLS - GDN_HEADS)).reshape(1, SMALL_COLS)

    xf = x.reshape(m, d)
    for l in range(depth):
        last = l == depth - 1
        xf = _ffn(xf, row(norm_ffn1_w[l]), ffn1_w_gate[l].astype(BF16), ffn1_w_up[l].astype(BF16),
                  ffn1_w_down[l].astype(BF16), row(norm_final_w), final_norm=False)
        w_main = w_in[l][:, :n_main].astype(BF16)
        w_small = jnp.pad(w_in[l][:, n_main:], ((0, 0), (0, SMALL_COLS - 2 * GDN_HEADS))).astype(BF16)
        proj, small = _inproj(xf, row(norm_mix_w[l]), w_main, w_small)
        ro = _retention(proj, pos3, invf2, sgn2, batch, seq)
        go = _gdn(proj, small, conv_w[l].astype(F32), pad_heads(gdn_a_log[l]), pad_heads(gdn_dt_bias[l]),
                  row(gdn_norm_w[l]), batch, seq)
        w_o = w_out[l].astype(BF16)
        xf = _outproj(xf, ro, go, w_o[:RET_WIDTH], w_o[RET_WIDTH:])
        xf = _ffn(xf, row(norm_ffn2_w[l]), ffn2_w_gate[l].astype(BF16), ffn2_w_up[l].astype(BF16),
                  ffn2_w_down[l].astype(BF16), row(norm_final_w), final_norm=last)
    return xf.reshape(batch, seq, d)
```

```python
import functools
import math

import jax
import jax.numpy as jnp
from jax import lax
from jax.experimental import pallas as pl
from jax.experimental.pallas import tpu as pltpu

F32 = jnp.float32
BF16 = jnp.bfloat16

D_MODEL = 2048
HEAD_DIM = 128
RET_HEADS = 8
GDN_HEADS = 8
RET_WIDTH = RET_HEADS * HEAD_DIM
GDN_WIDTH = GDN_HEADS * HEAD_DIM
D_FF = 5632
GDN_CHUNK = 64
CONV_K = 4
ROPE_THETA = 10000.0
EPS = 1e-6

V7X_LANES = 128
V7X_SUBLANES = 8
V7X_BF16_SUBLANES = 16
V7X_VMEM_BYTES = 64 * 2**20

FFN_TM, FFN_TF = 1024, 512
INPROJ_TM, INPROJ_TN = 1024, 1024
OUTPROJ_TM = 512
RET_TB = 256
GDN_TB = 256
CONV_ROWS = 32
SMALL_COLS = V7X_LANES


def _vmem_limit(pipelined_bytes, scratch_bytes, temp_bytes):
    need = 2 * pipelined_bytes + scratch_bytes + temp_bytes
    return int(min(need, V7X_VMEM_BYTES * 15 // 16))


def _mm(a, b):
    return jnp.dot(a.astype(BF16), b.astype(BF16), preferred_element_type=F32)


def _mm_nt(a, b):
    return lax.dot_general(a.astype(BF16), b.astype(BF16), (((1,), (1,)), ((), ())),
                           preferred_element_type=F32)


def _mm_tn(a, b):
    return lax.dot_general(a.astype(BF16), b.astype(BF16), (((0,), (0,)), ((), ())),
                           preferred_element_type=F32)


def _rmsnorm(x, w):
    return x * lax.rsqrt(jnp.mean(x * x, axis=-1, keepdims=True) + EPS) * w


def _silu(x):
    return x * jax.nn.sigmoid(x)


def _ffn_kernel(x_ref, nw_ref, wg_ref, wu_ref, wd_ref, fw_ref, *rest, final_norm, n_cast):
    cast_src = rest[:n_cast]
    o_ref = rest[n_cast]
    cast_dst = rest[n_cast + 1:2 * n_cast + 1]
    h_ref = rest[2 * n_cast + 1]
    j = pl.program_id(1)

    @pl.when(j == 0)
    def _():
        x = x_ref[...]
        h_ref[...] = _rmsnorm(x, nw_ref[...]).astype(BF16)
        o_ref[...] = x

    h = h_ref[...]
    g = jnp.dot(h, wg_ref[...], preferred_element_type=F32)
    u = jnp.dot(h, wu_ref[...], preferred_element_type=F32)
    act = (_silu(g) * u) * 0.5
    o_ref[...] += jnp.dot(act.astype(BF16), wd_ref[...], preferred_element_type=F32)
    for src, dst in zip(cast_src, cast_dst):
        dst[...] = src[...].astype(BF16)

    if final_norm:
        @pl.when(j == pl.num_programs(1) - 1)
        def _():
            o_ref[...] = _rmsnorm(o_ref[...], fw_ref[...])


def _cast_plan(w, n_steps):
    rows = w.shape[0]
    slab = V7X_BF16_SUBLANES * pl.cdiv(rows, V7X_BF16_SUBLANES * n_steps)
    assert rows % slab == 0, (w.shape, n_steps)
    return w, slab, rows // slab


def _ffn(x, norm_w, wg, wu, wd, final_w, cast_weights=(), *, final_norm):
    m, d = x.shape
    f = wg.shape[1]
    tm, tf = FFN_TM, FFN_TF
    ni, nj = m // tm, f // tf
    plans = [_cast_plan(w, ni * nj) for w in cast_weights]
    cast_specs = [pl.BlockSpec((r, v.shape[1]), lambda i, j, used=used: (jnp.minimum(i * nj + j, used - 1), 0))
                  for v, r, used in plans]
    cast_bytes = sum(r * v.shape[1] * (4 + 2) for v, r, _ in plans)
    blocks = 2 * tm * d * 4 + 3 * d * tf * 2 + 2 * d * 4 + cast_bytes
    outs = pl.pallas_call(
        functools.partial(_ffn_kernel, final_norm=final_norm, n_cast=len(plans)),
        grid=(ni, nj),
        in_specs=[
            pl.BlockSpec((tm, d), lambda i, j: (i, 0)),
            pl.BlockSpec((1, d), lambda i, j: (0, 0)),
            pl.BlockSpec((d, tf), lambda i, j: (0, j)),
            pl.BlockSpec((d, tf), lambda i, j: (0, j)),
            pl.BlockSpec((tf, d), lambda i, j: (j, 0)),
            pl.BlockSpec((1, d), lambda i, j: (0, 0)),
        ] + cast_specs,
        out_specs=[pl.BlockSpec((tm, d), lambda i, j: (i, 0))] + cast_specs,
        out_shape=[jax.ShapeDtypeStruct((m, d), F32)] + [jax.ShapeDtypeStruct(v.shape, BF16) for v, _, _ in plans],
        scratch_shapes=[pltpu.VMEM((tm, d), BF16)],
        compiler_params=pltpu.CompilerParams(
            dimension_semantics=("arbitrary", "arbitrary"),
            vmem_limit_bytes=_vmem_limit(blocks, tm * d * 2, 4 * tm * tf * 4)),
        name="ffn_final" if final_norm else "ffn",
    )(x, norm_w, wg, wu, wd, final_w, *[v for v, _, _ in plans])
    return outs[0], list(outs[1:])


def _inproj_kernel(x_ref, nw_ref, w_ref, ws_ref, o_ref, os_ref, h_ref):
    j = pl.program_id(1)

    @pl.when(j == 0)
    def _():
        hb = _rmsnorm(x_ref[...], nw_ref[...]).astype(BF16)
        h_ref[...] = hb
        os_ref[...] = jnp.dot(hb, ws_ref[...], preferred_element_type=F32)

    o_ref[...] = jnp.dot(h_ref[...], w_ref[...], preferred_element_type=F32)


def _inproj(x, norm_w, w_in, w_small, n_main):
    m, d = x.shape
    tm, tn = INPROJ_TM, INPROJ_TN
    blocks = tm * d * 4 + d * tn * 2 + d * SMALL_COLS * 2 + tm * tn * 4 + tm * SMALL_COLS * 4 + d * 4
    return pl.pallas_call(
        _inproj_kernel,
        grid=(m // tm, n_main // tn),
        in_specs=[
            pl.BlockSpec((tm, d), lambda i, j: (i, 0)),
            pl.BlockSpec((1, d), lambda i, j: (0, 0)),
            pl.BlockSpec((d, tn), lambda i, j: (0, j)),
            pl.BlockSpec((d, SMALL_COLS), lambda i, j: (0, 0)),
        ],
        out_specs=[
            pl.BlockSpec((tm, tn), lambda i, j: (i, j)),
            pl.BlockSpec((tm, SMALL_COLS), lambda i, j: (i, 0)),
        ],
        out_shape=[
            jax.ShapeDtypeStruct((m, n_main), F32),
            jax.ShapeDtypeStruct((m, SMALL_COLS), F32),
        ],
        scratch_shapes=[pltpu.VMEM((tm, d), BF16)],
        compiler_params=pltpu.CompilerParams(
            dimension_semantics=("parallel", "arbitrary"),
            vmem_limit_bytes=_vmem_limit(blocks, tm * d * 2, tm * d * 4 + tm * tn * 4)),
        name="inproj",
    )(x, norm_w, w_in, w_small)


def _retention_kernel(pos_ref, invf_ref, sgn_ref, q_ref, k_ref, v_ref, g_ref, o_ref,
                      state_ref, decay_ref, qdec_ref, kdec_ref, *, tb):
    heads = range(RET_HEADS)
    lanes = lambda h: slice(h * HEAD_DIM, (h + 1) * HEAD_DIM)
    log_gamma = [math.log(1.0 - 2.0 ** (-5 - h)) for h in heads]

    @pl.when((pl.program_id(0) == 0) & (pl.program_id(1) == 0))
    def _():
        ri = lax.broadcasted_iota(jnp.int32, (tb, tb), 0)
        ci = lax.broadcasted_iota(jnp.int32, (tb, tb), 1)
        causal = ri >= ci
        rel = jnp.where(causal, (ri - ci).astype(F32), 0.0)
        idx = lax.broadcasted_iota(jnp.int32, (tb, HEAD_DIM), 0).astype(F32)
        for h in heads:
            decay_ref[h] = jnp.where(causal, jnp.exp(log_gamma[h] * rel), 0.0)
            qdec_ref[h] = jnp.exp(log_gamma[h] * (idx + 1.0))
            kdec_ref[h] = jnp.exp(log_gamma[h] * (tb - 1.0 - idx))

    @pl.when(pl.program_id(1) == 0)
    def _():
        state_ref[...] = jnp.zeros_like(state_ref)

    pos = pos_ref[0].astype(F32)
    pos_rows = jnp.broadcast_to(pos, (V7X_LANES, tb)).T
    ang = pos_rows * invf_ref[...]
    cos2 = jnp.cos(ang)
    sin2 = jnp.sin(ang) * sgn_ref[...]
    scale = HEAD_DIM ** -0.5

    rot = lambda x: x * cos2 + pltpu.roll(x, HEAD_DIM // 2, 1) * sin2
    qr = [rot(q_ref[:, lanes(h)]) for h in heads]
    kr = [rot(k_ref[:, lanes(h)]) * scale for h in heads]
    v = [v_ref[:, lanes(h)] for h in heads]
    state = [state_ref[h] for h in heads]
    scores = [_mm_nt(qr[h], kr[h]) for h in heads]
    inter = [_mm(qr[h] * qdec_ref[h], state[h]) for h in heads]
    kv = [_mm_tn(kr[h] * kdec_ref[h], v[h]) for h in heads]
    intra = [_mm(scores[h] * decay_ref[h], v[h]) for h in heads]
    for h in heads:
        state_ref[h] = math.exp(log_gamma[h] * tb) * state[h] + kv[h]
        o = intra[h] + inter[h]
        mu = jnp.mean(o, axis=-1, keepdims=True)
        oc = o - mu
        var = jnp.mean(oc * oc, axis=-1, keepdims=True)
        o_ref[:, lanes(h)] = (oc * lax.rsqrt(var + EPS) * _silu(g_ref[:, lanes(h)])).astype(BF16)


def _retention(proj, pos3, invf2, sgn2, batch, seq):
    tb = RET_TB
    nt = seq // tb
    m = batch * seq
    col = lambda c: pl.BlockSpec((tb, RET_WIDTH), lambda b, t, c=c: (b * nt + t, c))
    blocks = 4 * tb * RET_WIDTH * 4 + tb * RET_WIDTH * 2 + tb * 4 + 2 * V7X_LANES * 4
    scratch = RET_HEADS * (HEAD_DIM * HEAD_DIM + tb * tb + 2 * tb * HEAD_DIM) * 4
    return pl.pallas_call(
        functools.partial(_retention_kernel, tb=tb),
        grid=(batch, nt),
        in_specs=[
            pl.BlockSpec((1, 1, tb), lambda b, t: (b * nt + t, 0, 0)),
            pl.BlockSpec((1, V7X_LANES), lambda b, t: (0, 0)),
            pl.BlockSpec((1, V7X_LANES), lambda b, t: (0, 0)),
            col(0), col(1), col(2), col(3),
        ],
        out_specs=pl.BlockSpec((tb, RET_WIDTH), lambda b, t: (b * nt + t, 0)),
        out_shape=jax.ShapeDtypeStruct((m, RET_WIDTH), BF16),
        scratch_shapes=[
            pltpu.VMEM((RET_HEADS, HEAD_DIM, HEAD_DIM), F32),
            pltpu.VMEM((RET_HEADS, tb, tb), F32),
            pltpu.VMEM((RET_HEADS, tb, HEAD_DIM), F32),
            pltpu.VMEM((RET_HEADS, tb, HEAD_DIM), F32),
        ],
        compiler_params=pltpu.CompilerParams(
            dimension_semantics=("arbitrary", "arbitrary"),
            vmem_limit_bytes=_vmem_limit(blocks, scratch, 16 * tb * tb * 4 + 8 * tb * RET_WIDTH * 4)),
        name="retention",
    )(pos3, invf2, sgn2, proj, proj, proj, proj)


def _softplus(x):
    return jnp.maximum(x, 0.0) + jnp.log1p(jnp.exp(-jnp.abs(x)))


def _unit_lower_inverses(ms, eye, blk16, blk32):
    d = [jnp.where(blk16, m, 0.0) for m in ms]
    d2 = [_mm(a, a) for a in d]
    d4 = [_mm(a, a) for a in d2]
    d8 = [_mm(a, a) for a in d4]
    p = [_mm(eye - a, eye + b) for a, b in zip(d, d2)]
    p = [_mm(a, eye + b) for a, b in zip(p, d4)]
    p = [_mm(a, eye + b) for a, b in zip(p, d8)]
    off16 = blk32 & jnp.logical_not(blk16)
    lp = [_mm(jnp.where(off16, m, 0.0), a) for m, a in zip(ms, p)]
    p = [a - _mm(a, b) for a, b in zip(p, lp)]
    lp = [_mm(jnp.where(blk32, 0.0, m), a) for m, a in zip(ms, p)]
    p = [a - _mm(a, b) for a, b in zip(p, lp)]
    return p


def _gdn_kernel(q_ref, k_ref, v_ref, z_ref, s_ref, cw_ref, alog_ref, dtb_ref, nw_ref,
                o_ref, pad_ref, conv_ref, state_ref, *, tb):
    c_len = GDN_CHUNK
    nc = tb // c_len
    halo = V7X_SUBLANES

    @pl.when(pl.program_id(1) == 0)
    def _():
        state_ref[...] = jnp.zeros_like(state_ref)
        pad_ref[:, 0:halo, :] = jnp.zeros((3, halo, GDN_WIDTH), F32)

    groups = [(0, halo)] + [(lo, min(CONV_ROWS, tb - lo)) for lo in range(halo, tb, CONV_ROWS)]
    for a, ref in enumerate((q_ref, k_ref, v_ref)):
        w = cw_ref[:, a * GDN_WIDTH:(a + 1) * GDN_WIDTH]
        taps = [w[CONV_K - 1 - back:CONV_K - back, :] for back in range(CONV_K)]
        pad_ref[a, halo:2 * halo, :] = ref[0:halo, :]
        for lo, n in groups:
            acc = ref[lo:lo + n, :] * taps[0]
            for back in range(1, CONV_K):
                prev = pad_ref[a, halo - back:2 * halo - back, :] if lo == 0 else ref[lo - back:lo - back + n, :]
                acc = acc + prev * taps[back]
            conv_ref[a, lo:lo + n, :] = _silu(acc)
        pad_ref[a, 0:halo, :] = ref[tb - halo:tb, :]

    small = s_ref[...]
    lane = lax.broadcasted_iota(jnp.int32, (tb, SMALL_COLS), 1)
    g_all = jnp.where(lane < GDN_HEADS, -jnp.exp(alog_ref[...]) * _softplus(small + dtb_ref[...]), 0.0)
    beta_all = jax.nn.sigmoid(small)

    ri = lax.broadcasted_iota(jnp.int32, (tb, tb), 0)
    ci = lax.broadcasted_iota(jnp.int32, (tb, tb), 1)
    chunk_tri = ((ri >= ci) & ((ri >> 6) == (ci >> 6))).astype(F32)
    gc = jnp.dot(chunk_tri, g_all, precision=lax.Precision.HIGHEST, preferred_element_type=F32)
    gct = gc.T

    r64 = lax.broadcasted_iota(jnp.int32, (c_len, c_len), 0)
    c64 = lax.broadcasted_iota(jnp.int32, (c_len, c_len), 1)
    causal = r64 >= c64
    strict = r64 > c64
    eye = (r64 == c64).astype(F32)
    blk16 = (r64 >> 4) == (c64 >> 4)
    blk32 = (r64 >> 5) == (c64 >> 5)
    scale = HEAD_DIM ** -0.5

    heads = range(GDN_HEADS)
    chains = [(h, c) for h in heads for c in range(nc)]
    rows = lambda c: slice(c * c_len, (c + 1) * c_len)
    lanes = lambda h: slice(h * HEAD_DIM, (h + 1) * HEAD_DIM)

    qn, kn, kb, vb, kbe, qe, g_b = [], [], [], [], [], [], []
    for h in heads:
        qh, kh, vh = conv_ref[0, :, lanes(h)], conv_ref[1, :, lanes(h)], conv_ref[2, :, lanes(h)]
        g_h = jnp.broadcast_to(gc[:, h:h + 1], (tb, HEAD_DIM))
        b_h = jnp.broadcast_to(beta_all[:, GDN_HEADS + h:GDN_HEADS + h + 1], (tb, HEAD_DIM))
        e_h = jnp.exp(g_h)
        q_h = qh * lax.rsqrt(jnp.sum(qh * qh, axis=-1, keepdims=True) + EPS) * scale
        k_h = kh * lax.rsqrt(jnp.sum(kh * kh, axis=-1, keepdims=True) + EPS)
        kb_h = k_h * b_h
        qn.append(q_h), kn.append(k_h), kb.append(kb_h), vb.append(vh * b_h)
        kbe.append(kb_h * e_h), qe.append(q_h * e_h), g_b.append(g_h)

    decay = [jnp.exp(jnp.where(causal, g_b[h][rows(c), :c_len] - gct[h:h + 1, rows(c)], -jnp.inf))
             for h, c in chains]
    kq = [_mm_nt(jnp.concatenate([kb[h][rows(c)], qn[h][rows(c)]], axis=0), kn[h][rows(c)])
          for h, c in chains]
    ms = [jnp.where(strict, a[:c_len] * dc, 0.0) for a, dc in zip(kq, decay)]
    attn = [a[c_len:] * dc for a, dc in zip(kq, decay)]
    t_inv = _unit_lower_inverses(ms, eye, blk16, blk32)
    rhs = [jnp.concatenate([vb[h][rows(c)], kbe[h][rows(c)]], axis=1) for h, c in chains]
    sol = [r + _mm(t - eye, r) for t, r in zip(t_inv, rhs)]

    state = [state_ref[h] for h in heads]
    outs = [[] for _ in heads]
    for c in range(nc):
        last = (c + 1) * c_len - 1
        g_last = [g_b[h][last:last + 1, :] for h in heads]
        k_dec = [kn[h][rows(c)] * jnp.exp(g_last[h] - g_b[h][rows(c)]) for h in heads]
        w_s = [_mm(sol[h * nc + c][:, HEAD_DIM:], state[h]) for h in heads]
        q_s = [_mm(qe[h][rows(c)], state[h]) for h in heads]
        v_new = [sol[h * nc + c][:, :HEAD_DIM] - w_s[h] for h in heads]
        a_v = [_mm(attn[h * nc + c], v_new[h]) for h in heads]
        k_v = [_mm_tn(k_dec[h], v_new[h]) for h in heads]
        state = [state[h] * jnp.exp(g_last[h]) + k_v[h] for h in heads]
        for h in heads:
            outs[h].append(q_s[h] + a_v[h])

    for h in heads:
        state_ref[h] = state[h]
        o = jnp.concatenate(outs[h], axis=0)
        o = o * lax.rsqrt(jnp.mean(o * o, axis=-1, keepdims=True) + EPS) * nw_ref[...]
        o_ref[:, lanes(h)] = (o * _silu(z_ref[:, lanes(h)])).astype(BF16)


def _gdn(proj, small, conv_w, alog_row, dtb_row, norm_w, batch, seq):
    tb = GDN_TB
    nt = seq // tb
    m = batch * seq
    first = RET_WIDTH * 4 // GDN_WIDTH
    col = lambda c: pl.BlockSpec((tb, GDN_WIDTH), lambda b, t, c=c: (b * nt + t, first + c))
    row = lambda n: pl.BlockSpec((1, n), lambda b, t: (0, 0))
    blocks = (4 * tb * GDN_WIDTH * 4 + tb * SMALL_COLS * 4 + CONV_K * 3 * GDN_WIDTH * 4
              + tb * GDN_WIDTH * 2 + 3 * V7X_LANES * 4)
    scratch = (3 * 2 * V7X_SUBLANES * GDN_WIDTH * 4 + 3 * tb * GDN_WIDTH * 4
               + GDN_HEADS * HEAD_DIM * HEAD_DIM * 4)
    return pl.pallas_call(
        functools.partial(_gdn_kernel, tb=tb),
        grid=(batch, nt),
        in_specs=[
            col(0), col(1), col(2), col(3),
            pl.BlockSpec((tb, SMALL_COLS), lambda b, t: (b * nt + t, 0)),
            pl.BlockSpec((CONV_K, 3 * GDN_WIDTH), lambda b, t: (0, 0)),
            row(SMALL_COLS), row(SMALL_COLS), row(HEAD_DIM),
        ],
        out_specs=pl.BlockSpec((tb, GDN_WIDTH), lambda b, t: (b * nt + t, 0)),
        out_shape=jax.ShapeDtypeStruct((m, GDN_WIDTH), BF16),
        scratch_shapes=[
            pltpu.VMEM((3, 2 * V7X_SUBLANES, GDN_WIDTH), F32),
            pltpu.VMEM((3, tb, GDN_WIDTH), F32),
            pltpu.VMEM((GDN_HEADS, HEAD_DIM, HEAD_DIM), F32),
        ],
        compiler_params=pltpu.CompilerParams(
            dimension_semantics=("arbitrary", "arbitrary"),
            vmem_limit_bytes=_vmem_limit(blocks, scratch, 12 * tb * GDN_WIDTH * 4)),
        name="gdn",
    )(proj, proj, proj, proj, small, conv_w, alog_row, dtb_row, norm_w)


def _outproj_kernel(x_ref, ro_ref, go_ref, wr_ref, wg_ref, o_ref):
    o_ref[...] = (x_ref[...]
                  + jnp.dot(ro_ref[...], wr_ref[...], preferred_element_type=F32)
                  + jnp.dot(go_ref[...], wg_ref[...], preferred_element_type=F32))


def _outproj(x, ro, go, w_out):
    m, d = x.shape
    tm = OUTPROJ_TM
    blocks = 2 * tm * d * 4 + tm * (RET_WIDTH + GDN_WIDTH) * 2 + (RET_WIDTH + GDN_WIDTH) * d * 2
    return pl.pallas_call(
        _outproj_kernel,
        grid=(m // tm,),
        in_specs=[
            pl.BlockSpec((tm, d), lambda i: (i, 0)),
            pl.BlockSpec((tm, RET_WIDTH), lambda i: (i, 0)),
            pl.BlockSpec((tm, GDN_WIDTH), lambda i: (i, 0)),
            pl.BlockSpec((RET_WIDTH, d), lambda i: (0, 0)),
            pl.BlockSpec((GDN_WIDTH, d), lambda i: (RET_WIDTH // GDN_WIDTH, 0)),
        ],
        out_specs=pl.BlockSpec((tm, d), lambda i: (i, 0)),
        out_shape=jax.ShapeDtypeStruct((m, d), F32),
        compiler_params=pltpu.CompilerParams(
            dimension_semantics=("parallel",),
            vmem_limit_bytes=_vmem_limit(blocks, 0, 2 * tm * d * 4)),
        name="outproj",
    )(x, ro, go, w_out, w_out)


def kernel(x, positions, norm_ffn1_w, ffn1_w_gate, ffn1_w_up, ffn1_w_down, norm_mix_w, w_in, conv_w,
           gdn_a_log, gdn_dt_bias, gdn_norm_w, w_out, norm_ffn2_w, ffn2_w_gate, ffn2_w_up, ffn2_w_down,
           norm_final_w):
    batch, seq, d = x.shape
    depth = norm_ffn1_w.shape[0]
    m = batch * seq
    n_main = 4 * RET_WIDTH + 4 * GDN_WIDTH
    row = lambda v: v.reshape(1, -1).astype(F32)

    half = HEAD_DIM // 2
    inv_freq = ROPE_THETA ** (-jnp.arange(half, dtype=F32) / half)
    invf2 = jnp.concatenate([inv_freq, inv_freq]).reshape(1, HEAD_DIM)
    sgn2 = jnp.concatenate([-jnp.ones((half,), F32), jnp.ones((half,), F32)]).reshape(1, HEAD_DIM)
    pos3 = positions.reshape(m // RET_TB, 1, RET_TB)
    pad_heads = lambda v: jnp.pad(v.astype(F32), (0, SMALL_COLS - GDN_HEADS)).reshape(1, SMALL_COLS)

    xf = x.reshape(m, d)
    for l in range(depth):
        last = l == depth - 1
        later = (w_in[l], w_out[l], ffn2_w_gate[l], ffn2_w_up[l], ffn2_w_down[l])
        xf, (w_in_b, w_out_b, wg2, wu2, wd2) = _ffn(
            xf, row(norm_ffn1_w[l]), ffn1_w_gate[l].astype(BF16), ffn1_w_up[l].astype(BF16),
            ffn1_w_down[l].astype(BF16), row(norm_final_w), later, final_norm=False)
        w_small = jnp.pad(w_in_b[:, n_main:], ((0, 0), (0, SMALL_COLS - 2 * GDN_HEADS)))
        proj, small = _inproj(xf, row(norm_mix_w[l]), w_in_b, w_small, n_main)
        ro = _retention(proj, pos3, invf2, sgn2, batch, seq)
        go = _gdn(proj, small, conv_w[l].astype(F32), pad_heads(gdn_a_log[l]), pad_heads(gdn_dt_bias[l]),
                  row(gdn_norm_w[l]), batch, seq)
        xf = _outproj(xf, ro, go, w_out_b)
        xf, _ = _ffn(xf, row(norm_ffn2_w[l]), wg2, wu2, wd2, row(norm_final_w), final_norm=last)
    return xf.reshape(batch, seq, d)
```

```python
import functools
import math

import jax
import jax.numpy as jnp
from jax import lax
from jax.experimental import pallas as pl
from jax.experimental.pallas import tpu as pltpu

F32 = jnp.float32
BF16 = jnp.bfloat16

D_MODEL = 2048
HEAD_DIM = 128
RET_HEADS = 8
GDN_HEADS = 8
RET_WIDTH = RET_HEADS * HEAD_DIM
GDN_WIDTH = GDN_HEADS * HEAD_DIM
D_FF = 5632
GDN_CHUNK = 64
CONV_K = 4
ROPE_THETA = 10000.0
EPS = 1e-6

V7X_LANES = 128
V7X_SUBLANES = 8
V7X_BF16_SUBLANES = 16
V7X_VMEM_BYTES = 64 * 2**20

FFN_TM, FFN_TF = 1024, 512
INPROJ_TM, INPROJ_TN = 1024, 2048
OUTPROJ_TM = 512
RET_TB = 256
GDN_TB = 256
CONV_ROWS = 32
SMALL_COLS = V7X_LANES


def _vmem_limit(pipelined_bytes, scratch_bytes, temp_bytes):
    need = 2 * pipelined_bytes + scratch_bytes + temp_bytes
    return int(min(need, V7X_VMEM_BYTES * 15 // 16))


def _mm(a, b):
    return jnp.dot(a.astype(BF16), b.astype(BF16), preferred_element_type=F32)


def _mm_nt(a, b):
    return lax.dot_general(a.astype(BF16), b.astype(BF16), (((1,), (1,)), ((), ())),
                           preferred_element_type=F32)


def _mm_tn(a, b):
    return lax.dot_general(a.astype(BF16), b.astype(BF16), (((0,), (0,)), ((), ())),
                           preferred_element_type=F32)


def _rmsnorm(x, w):
    return x * lax.rsqrt(jnp.mean(x * x, axis=-1, keepdims=True) + EPS) * w


def _silu(x):
    return x * jax.nn.sigmoid(x)


def _ffn_kernel(x_ref, nw_ref, wg_ref, wu_ref, wd_ref, fw_ref, *rest, final_norm, n_cast):
    cast_src = rest[:n_cast]
    o_ref = rest[n_cast]
    cast_dst = rest[n_cast + 1:2 * n_cast + 1]
    h_ref = rest[2 * n_cast + 1]
    j = pl.program_id(1)

    @pl.when(j == 0)
    def _():
        x = x_ref[...]
        h_ref[...] = _rmsnorm(x, nw_ref[...]).astype(BF16)
        o_ref[...] = x

    h = h_ref[...]
    g = jnp.dot(h, wg_ref[...], preferred_element_type=F32)
    u = jnp.dot(h, wu_ref[...], preferred_element_type=F32)
    act = (_silu(g) * u) * 0.5
    o_ref[...] += jnp.dot(act.astype(BF16), wd_ref[...], preferred_element_type=F32)
    for src, dst in zip(cast_src, cast_dst):
        dst[...] = src[...].astype(BF16)

    if final_norm:
        @pl.when(j == pl.num_programs(1) - 1)
        def _():
            o_ref[...] = _rmsnorm(o_ref[...], fw_ref[...])


def _cast_plan(w, n_steps):
    rows = w.shape[0]
    slab = V7X_BF16_SUBLANES * pl.cdiv(rows, V7X_BF16_SUBLANES * n_steps)
    assert rows % slab == 0, (w.shape, n_steps)
    return w, slab, rows // slab


def _ffn(x, norm_w, wg, wu, wd, final_w, cast_weights=(), *, final_norm):
    m, d = x.shape
    f = wg.shape[1]
    tm, tf = FFN_TM, FFN_TF
    ni, nj = m // tm, f // tf
    plans = [_cast_plan(w, ni * nj) for w in cast_weights]
    cast_specs = [pl.BlockSpec((r, v.shape[1]), lambda i, j, used=used: (jnp.minimum(i * nj + j, used - 1), 0))
                  for v, r, used in plans]
    cast_bytes = sum(r * v.shape[1] * (4 + 2) for v, r, _ in plans)
    blocks = 2 * tm * d * 4 + 3 * d * tf * 2 + 2 * d * 4 + cast_bytes
    outs = pl.pallas_call(
        functools.partial(_ffn_kernel, final_norm=final_norm, n_cast=len(plans)),
        grid=(ni, nj),
        in_specs=[
            pl.BlockSpec((tm, d), lambda i, j: (i, 0)),
            pl.BlockSpec((1, d), lambda i, j: (0, 0)),
            pl.BlockSpec((d, tf), lambda i, j: (0, j)),
            pl.BlockSpec((d, tf), lambda i, j: (0, j)),
            pl.BlockSpec((tf, d), lambda i, j: (j, 0)),
            pl.BlockSpec((1, d), lambda i, j: (0, 0)),
        ] + cast_specs,
        out_specs=[pl.BlockSpec((tm, d), lambda i, j: (i, 0))] + cast_specs,
        out_shape=[jax.ShapeDtypeStruct((m, d), F32)] + [jax.ShapeDtypeStruct(v.shape, BF16) for v, _, _ in plans],
        scratch_shapes=[pltpu.VMEM((tm, d), BF16)],
        compiler_params=pltpu.CompilerParams(
            dimension_semantics=("arbitrary", "arbitrary"),
            vmem_limit_bytes=_vmem_limit(blocks, tm * d * 2, 4 * tm * tf * 4)),
        name="ffn_final" if final_norm else "ffn",
    )(x, norm_w, wg, wu, wd, final_w, *[v for v, _, _ in plans])
    return outs[0], list(outs[1:])


def _inproj_kernel(x_ref, nw_ref, wt_ref, wst_ref, o_ref, os_ref, h_ref):
    j = pl.program_id(1)

    @pl.when(j == 0)
    def _():
        hb = _rmsnorm(x_ref[...], nw_ref[...]).astype(BF16)
        h_ref[...] = hb
        os_ref[...] = lax.dot_general(hb, wst_ref[...], (((1,), (1,)), ((), ())), preferred_element_type=F32)

    o_ref[...] = lax.dot_general(h_ref[...], wt_ref[...], (((1,), (1,)), ((), ())), preferred_element_type=F32)


def _inproj(x, norm_w, w_in_t, w_small_t, n_main):
    m, d = x.shape
    tm, tn = INPROJ_TM, INPROJ_TN
    blocks = tm * d * 4 + d * tn * 2 + d * SMALL_COLS * 2 + tm * tn * 4 + tm * SMALL_COLS * 4 + d * 4
    return pl.pallas_call(
        _inproj_kernel,
        grid=(m // tm, n_main // tn),
        in_specs=[
            pl.BlockSpec((tm, d), lambda i, j: (i, 0)),
            pl.BlockSpec((1, d), lambda i, j: (0, 0)),
            pl.BlockSpec((tn, d), lambda i, j: (j, 0)),
            pl.BlockSpec((SMALL_COLS, d), lambda i, j: (0, 0)),
        ],
        out_specs=[
            pl.BlockSpec((tm, tn), lambda i, j: (i, j)),
            pl.BlockSpec((tm, SMALL_COLS), lambda i, j: (i, 0)),
        ],
        out_shape=[
            jax.ShapeDtypeStruct((m, n_main), F32),
            jax.ShapeDtypeStruct((m, SMALL_COLS), F32),
        ],
        scratch_shapes=[pltpu.VMEM((tm, d), BF16)],
        compiler_params=pltpu.CompilerParams(
            dimension_semantics=("parallel", "arbitrary"),
            vmem_limit_bytes=_vmem_limit(blocks, tm * d * 2, tm * d * 4 + tm * tn * 4)),
        name="inproj",
    )(x, norm_w, w_in_t, w_small_t)


def _retention_kernel(pos_ref, invf_ref, sgn_ref, q_ref, k_ref, v_ref, g_ref, o_ref,
                      state_ref, decay_ref, qdec_ref, kdec_ref, *, tb):
    heads = range(RET_HEADS)
    lanes = lambda h: slice(h * HEAD_DIM, (h + 1) * HEAD_DIM)
    log_gamma = [math.log(1.0 - 2.0 ** (-5 - h)) for h in heads]

    @pl.when((pl.program_id(0) == 0) & (pl.program_id(1) == 0))
    def _():
        ri = lax.broadcasted_iota(jnp.int32, (tb, tb), 0)
        ci = lax.broadcasted_iota(jnp.int32, (tb, tb), 1)
        causal = ri >= ci
        rel = jnp.where(causal, (ri - ci).astype(F32), 0.0)
        idx = lax.broadcasted_iota(jnp.int32, (tb, HEAD_DIM), 0).astype(F32)
        for h in heads:
            decay_ref[h] = jnp.where(causal, jnp.exp(log_gamma[h] * rel), 0.0)
            qdec_ref[h] = jnp.exp(log_gamma[h] * (idx + 1.0))
            kdec_ref[h] = jnp.exp(log_gamma[h] * (tb - 1.0 - idx))

    @pl.when(pl.program_id(1) == 0)
    def _():
        state_ref[...] = jnp.zeros_like(state_ref)

    pos = pos_ref[0].astype(F32)
    pos_rows = jnp.broadcast_to(pos, (V7X_LANES, tb)).T
    ang = pos_rows * invf_ref[...]
    cos2 = jnp.cos(ang)
    sin2 = jnp.sin(ang) * sgn_ref[...]
    scale = HEAD_DIM ** -0.5

    rot = lambda x: x * cos2 + pltpu.roll(x, HEAD_DIM // 2, 1) * sin2
    qr = [rot(q_ref[:, lanes(h)]) for h in heads]
    kr = [rot(k_ref[:, lanes(h)]) * scale for h in heads]
    v = [v_ref[:, lanes(h)] for h in heads]
    state = [state_ref[h] for h in heads]
    scores = [_mm_nt(qr[h], kr[h]) for h in heads]
    inter = [_mm(qr[h] * qdec_ref[h], state[h]) for h in heads]
    kv = [_mm_tn(kr[h] * kdec_ref[h], v[h]) for h in heads]
    intra = [_mm(scores[h] * decay_ref[h], v[h]) for h in heads]
    for h in heads:
        state_ref[h] = math.exp(log_gamma[h] * tb) * state[h] + kv[h]
        o = intra[h] + inter[h]
        mu = jnp.mean(o, axis=-1, keepdims=True)
        oc = o - mu
        var = jnp.mean(oc * oc, axis=-1, keepdims=True)
        o_ref[:, lanes(h)] = (oc * lax.rsqrt(var + EPS) * _silu(g_ref[:, lanes(h)])).astype(BF16)


def _retention(proj, pos3, invf2, sgn2, batch, seq):
    tb = RET_TB
    nt = seq // tb
    m = batch * seq
    col = lambda c: pl.BlockSpec((tb, RET_WIDTH), lambda b, t, c=c: (b * nt + t, c))
    blocks = 4 * tb * RET_WIDTH * 4 + tb * RET_WIDTH * 2 + tb * 4 + 2 * V7X_LANES * 4
    scratch = RET_HEADS * (HEAD_DIM * HEAD_DIM + tb * tb + 2 * tb * HEAD_DIM) * 4
    return pl.pallas_call(
        functools.partial(_retention_kernel, tb=tb),
        grid=(batch, nt),
        in_specs=[
            pl.BlockSpec((1, 1, tb), lambda b, t: (b * nt + t, 0, 0)),
            pl.BlockSpec((1, V7X_LANES), lambda b, t: (0, 0)),
            pl.BlockSpec((1, V7X_LANES), lambda b, t: (0, 0)),
            col(0), col(1), col(2), col(3),
        ],
        out_specs=pl.BlockSpec((tb, RET_WIDTH), lambda b, t: (b * nt + t, 0)),
        out_shape=jax.ShapeDtypeStruct((m, RET_WIDTH), BF16),
        scratch_shapes=[
            pltpu.VMEM((RET_HEADS, HEAD_DIM, HEAD_DIM), F32),
            pltpu.VMEM((RET_HEADS, tb, tb), F32),
            pltpu.VMEM((RET_HEADS, tb, HEAD_DIM), F32),
            pltpu.VMEM((RET_HEADS, tb, HEAD_DIM), F32),
        ],
        compiler_params=pltpu.CompilerParams(
            dimension_semantics=("arbitrary", "arbitrary"),
            vmem_limit_bytes=_vmem_limit(blocks, scratch, 16 * tb * tb * 4 + 8 * tb * RET_WIDTH * 4)),
        name="retention",
    )(pos3, invf2, sgn2, proj, proj, proj, proj)


def _softplus(x):
    return jnp.maximum(x, 0.0) + jnp.log1p(jnp.exp(-jnp.abs(x)))


def _unit_lower_inverses(ms, eye, blk16, blk32):
    d = [jnp.where(blk16, m, 0.0) for m in ms]
    d2 = [_mm(a, a) for a in d]
    d4 = [_mm(a, a) for a in d2]
    d8 = [_mm(a, a) for a in d4]
    p = [_mm(eye - a, eye + b) for a, b in zip(d, d2)]
    p = [_mm(a, eye + b) for a, b in zip(p, d4)]
    p = [_mm(a, eye + b) for a, b in zip(p, d8)]
    off16 = blk32 & jnp.logical_not(blk16)
    lp = [_mm(jnp.where(off16, m, 0.0), a) for m, a in zip(ms, p)]
    p = [a - _mm(a, b) for a, b in zip(p, lp)]
    lp = [_mm(jnp.where(blk32, 0.0, m), a) for m, a in zip(ms, p)]
    p = [a - _mm(a, b) for a, b in zip(p, lp)]
    return p


def _gdn_kernel(q_ref, k_ref, v_ref, z_ref, s_ref, cw_ref, alog_ref, dtb_ref, nw_ref,
                o_ref, pad_ref, conv_ref, state_ref, *, tb):
    c_len = GDN_CHUNK
    nc = tb // c_len
    halo = V7X_SUBLANES

    @pl.when(pl.program_id(1) == 0)
    def _():
        state_ref[...] = jnp.zeros_like(state_ref)
        pad_ref[:, 0:halo, :] = jnp.zeros((3, halo, GDN_WIDTH), F32)

    groups = [(0, halo)] + [(lo, min(CONV_ROWS, tb - lo)) for lo in range(halo, tb, CONV_ROWS)]
    for a, ref in enumerate((q_ref, k_ref, v_ref)):
        w = cw_ref[:, a * GDN_WIDTH:(a + 1) * GDN_WIDTH]
        taps = [w[CONV_K - 1 - back:CONV_K - back, :] for back in range(CONV_K)]
        pad_ref[a, halo:2 * halo, :] = ref[0:halo, :]
        for lo, n in groups:
            acc = ref[lo:lo + n, :] * taps[0]
            for back in range(1, CONV_K):
                prev = pad_ref[a, halo - back:2 * halo - back, :] if lo == 0 else ref[lo - back:lo - back + n, :]
                acc = acc + prev * taps[back]
            conv_ref[a, lo:lo + n, :] = _silu(acc)
        pad_ref[a, 0:halo, :] = ref[tb - halo:tb, :]

    small = s_ref[...]
    lane = lax.broadcasted_iota(jnp.int32, (tb, SMALL_COLS), 1)
    g_all = jnp.where(lane < GDN_HEADS, -jnp.exp(alog_ref[...]) * _softplus(small + dtb_ref[...]), 0.0)
    beta_all = jax.nn.sigmoid(small)

    ri = lax.broadcasted_iota(jnp.int32, (tb, tb), 0)
    ci = lax.broadcasted_iota(jnp.int32, (tb, tb), 1)
    chunk_tri = ((ri >= ci) & ((ri >> 6) == (ci >> 6))).astype(F32)
    gc = jnp.dot(chunk_tri, g_all, precision=lax.Precision.HIGHEST, preferred_element_type=F32)
    gct = gc.T

    r64 = lax.broadcasted_iota(jnp.int32, (c_len, c_len), 0)
    c64 = lax.broadcasted_iota(jnp.int32, (c_len, c_len), 1)
    causal = r64 >= c64
    strict = r64 > c64
    eye = (r64 == c64).astype(F32)
    blk16 = (r64 >> 4) == (c64 >> 4)
    blk32 = (r64 >> 5) == (c64 >> 5)
    scale = HEAD_DIM ** -0.5

    heads = range(GDN_HEADS)
    chains = [(h, c) for h in heads for c in range(nc)]
    rows = lambda c: slice(c * c_len, (c + 1) * c_len)
    lanes = lambda h: slice(h * HEAD_DIM, (h + 1) * HEAD_DIM)

    qn, kn, kb, vb, kbe, qe, g_b = [], [], [], [], [], [], []
    for h in heads:
        qh, kh, vh = conv_ref[0, :, lanes(h)], conv_ref[1, :, lanes(h)], conv_ref[2, :, lanes(h)]
        g_h = jnp.broadcast_to(gc[:, h:h + 1], (tb, HEAD_DIM))
        b_h = jnp.broadcast_to(beta_all[:, GDN_HEADS + h:GDN_HEADS + h + 1], (tb, HEAD_DIM))
        e_h = jnp.exp(g_h)
        q_h = qh * lax.rsqrt(jnp.sum(qh * qh, axis=-1, keepdims=True) + EPS) * scale
        k_h = kh * lax.rsqrt(jnp.sum(kh * kh, axis=-1, keepdims=True) + EPS)
        kb_h = k_h * b_h
        qn.append(q_h), kn.append(k_h), kb.append(kb_h), vb.append(vh * b_h)
        kbe.append(kb_h * e_h), qe.append(q_h * e_h), g_b.append(g_h)

    decay = [jnp.exp(jnp.where(causal, g_b[h][rows(c), :c_len] - gct[h:h + 1, rows(c)], -jnp.inf))
             for h, c in chains]
    kq = [_mm_nt(jnp.concatenate([kb[h][rows(c)], qn[h][rows(c)]], axis=0), kn[h][rows(c)])
          for h, c in chains]
    ms = [jnp.where(strict, a[:c_len] * dc, 0.0) for a, dc in zip(kq, decay)]
    attn = [a[c_len:] * dc for a, dc in zip(kq, decay)]
    t_inv = _unit_lower_inverses(ms, eye, blk16, blk32)
    rhs = [jnp.concatenate([vb[h][rows(c)], kbe[h][rows(c)]], axis=1) for h, c in chains]
    sol = [r + _mm(t - eye, r) for t, r in zip(t_inv, rhs)]

    state = [state_ref[h] for h in heads]
    outs = [[] for _ in heads]
    for c in range(nc):
        last = (c + 1) * c_len - 1
        g_last = [g_b[h][last:last + 1, :] for h in heads]
        k_dec = [kn[h][rows(c)] * jnp.exp(g_last[h] - g_b[h][rows(c)]) for h in heads]
        w_s = [_mm(sol[h * nc + c][:, HEAD_DIM:], state[h]) for h in heads]
        q_s = [_mm(qe[h][rows(c)], state[h]) for h in heads]
        v_new = [sol[h * nc + c][:, :HEAD_DIM] - w_s[h] for h in heads]
        a_v = [_mm(attn[h * nc + c], v_new[h]) for h in heads]
        k_v = [_mm_tn(k_dec[h], v_new[h]) for h in heads]
        state = [state[h] * jnp.exp(g_last[h]) + k_v[h] for h in heads]
        for h in heads:
            outs[h].append(q_s[h] + a_v[h])

    for h in heads:
        state_ref[h] = state[h]
        o = jnp.concatenate(outs[h], axis=0)
        o = o * lax.rsqrt(jnp.mean(o * o, axis=-1, keepdims=True) + EPS) * nw_ref[...]
        o_ref[:, lanes(h)] = (o * _silu(z_ref[:, lanes(h)])).astype(BF16)


def _gdn(proj, small, conv_w, alog_row, dtb_row, norm_w, batch, seq):
    tb = GDN_TB
    nt = seq // tb
    m = batch * seq
    first = RET_WIDTH * 4 // GDN_WIDTH
    col = lambda c: pl.BlockSpec((tb, GDN_WIDTH), lambda b, t, c=c: (b * nt + t, first + c))
    row = lambda n: pl.BlockSpec((1, n), lambda b, t: (0, 0))
    blocks = (4 * tb * GDN_WIDTH * 4 + tb * SMALL_COLS * 4 + CONV_K * 3 * GDN_WIDTH * 4
              + tb * GDN_WIDTH * 2 + 3 * V7X_LANES * 4)
    scratch = (3 * 2 * V7X_SUBLANES * GDN_WIDTH * 4 + 3 * tb * GDN_WIDTH * 4
               + GDN_HEADS * HEAD_DIM * HEAD_DIM * 4)
    return pl.pallas_call(
        functools.partial(_gdn_kernel, tb=tb),
        grid=(batch, nt),
        in_specs=[
            col(0), col(1), col(2), col(3),
            pl.BlockSpec((tb, SMALL_COLS), lambda b, t: (b * nt + t, 0)),
            pl.BlockSpec((CONV_K, 3 * GDN_WIDTH), lambda b, t: (0, 0)),
            row(SMALL_COLS), row(SMALL_COLS), row(HEAD_DIM),
        ],
        out_specs=pl.BlockSpec((tb, GDN_WIDTH), lambda b, t: (b * nt + t, 0)),
        out_shape=jax.ShapeDtypeStruct((m, GDN_WIDTH), BF16),
        scratch_shapes=[
            pltpu.VMEM((3, 2 * V7X_SUBLANES, GDN_WIDTH), F32),
            pltpu.VMEM((3, tb, GDN_WIDTH), F32),
            pltpu.VMEM((GDN_HEADS, HEAD_DIM, HEAD_DIM), F32),
        ],
        compiler_params=pltpu.CompilerParams(
            dimension_semantics=("arbitrary", "arbitrary"),
            vmem_limit_bytes=_vmem_limit(blocks, scratch, 12 * tb * GDN_WIDTH * 4)),
        name="gdn",
    )(proj, proj, proj, proj, small, conv_w, alog_row, dtb_row, norm_w)


def _outproj_kernel(x_ref, ro_ref, go_ref, wr_ref, wg_ref, o_ref):
    o_ref[...] = (x_ref[...]
                  + jnp.dot(ro_ref[...], wr_ref[...], preferred_element_type=F32)
                  + jnp.dot(go_ref[...], wg_ref[...], preferred_element_type=F32))


def _outproj(x, ro, go, w_out):
    m, d = x.shape
    tm = OUTPROJ_TM
    blocks = 2 * tm * d * 4 + tm * (RET_WIDTH + GDN_WIDTH) * 2 + (RET_WIDTH + GDN_WIDTH) * d * 2
    return pl.pallas_call(
        _outproj_kernel,
        grid=(m // tm,),
        in_specs=[
            pl.BlockSpec((tm, d), lambda i: (i, 0)),
            pl.BlockSpec((tm, RET_WIDTH), lambda i: (i, 0)),
            pl.BlockSpec((tm, GDN_WIDTH), lambda i: (i, 0)),
            pl.BlockSpec((RET_WIDTH, d), lambda i: (0, 0)),
            pl.BlockSpec((GDN_WIDTH, d), lambda i: (RET_WIDTH // GDN_WIDTH, 0)),
        ],
        out_specs=pl.BlockSpec((tm, d), lambda i: (i, 0)),
        out_shape=jax.ShapeDtypeStruct((m, d), F32),
        compiler_params=pltpu.CompilerParams(
            dimension_semantics=("parallel",),
            vmem_limit_bytes=_vmem_limit(blocks, 0, 2 * tm * d * 4)),
        name="outproj",
    )(x, ro, go, w_out, w_out)


def kernel(x, positions, norm_ffn1_w, ffn1_w_gate, ffn1_w_up, ffn1_w_down, norm_mix_w, w_in, conv_w,
           gdn_a_log, gdn_dt_bias, gdn_norm_w, w_out, norm_ffn2_w, ffn2_w_gate, ffn2_w_up, ffn2_w_down,
           norm_final_w):
    batch, seq, d = x.shape
    depth = norm_ffn1_w.shape[0]
    m = batch * seq
    n_main = 4 * RET_WIDTH + 4 * GDN_WIDTH
    row = lambda v: v.reshape(1, -1).astype(F32)

    half = HEAD_DIM // 2
    inv_freq = ROPE_THETA ** (-jnp.arange(half, dtype=F32) / half)
    invf2 = jnp.concatenate([inv_freq, inv_freq]).reshape(1, HEAD_DIM)
    sgn2 = jnp.concatenate([-jnp.ones((half,), F32), jnp.ones((half,), F32)]).reshape(1, HEAD_DIM)
    pos3 = positions.reshape(m // RET_TB, 1, RET_TB)
    pad_heads = lambda v: jnp.pad(v.astype(F32), (0, SMALL_COLS - GDN_HEADS)).reshape(1, SMALL_COLS)

    xf = x.reshape(m, d)
    for l in range(depth):
        last = l == depth - 1
        later = (jnp.swapaxes(w_in[l], 0, 1), w_out[l], ffn2_w_gate[l], ffn2_w_up[l], ffn2_w_down[l])
        xf, (w_in_b, w_out_b, wg2, wu2, wd2) = _ffn(
            xf, row(norm_ffn1_w[l]), ffn1_w_gate[l].astype(BF16), ffn1_w_up[l].astype(BF16),
            ffn1_w_down[l].astype(BF16), row(norm_final_w), later, final_norm=False)
        w_small = jnp.pad(w_in_b[n_main:], ((0, SMALL_COLS - 2 * GDN_HEADS), (0, 0)))
        proj, small = _inproj(xf, row(norm_mix_w[l]), w_in_b, w_small, n_main)
        ro = _retention(proj, pos3, invf2, sgn2, batch, seq)
        go = _gdn(proj, small, conv_w[l].astype(F32), pad_heads(gdn_a_log[l]), pad_heads(gdn_dt_bias[l]),
                  row(gdn_norm_w[l]), batch, seq)
        xf = _outproj(xf, ro, go, w_out_b)
        xf, _ = _ffn(xf, row(norm_ffn2_w[l]), wg2, wu2, wd2, row(norm_final_w), final_norm=last)
    return xf.reshape(batch, seq, d)
```

```python
import functools
import math

import jax
import jax.numpy as jnp
from jax import lax
from jax.experimental import pallas as pl
from jax.experimental.pallas import tpu as pltpu

F32 = jnp.float32
BF16 = jnp.bfloat16

D_MODEL = 2048
HEAD_DIM = 128
RET_HEADS = 8
GDN_HEADS = 8
RET_WIDTH = RET_HEADS * HEAD_DIM
GDN_WIDTH = GDN_HEADS * HEAD_DIM
D_FF = 5632
GDN_CHUNK = 64
CONV_K = 4
ROPE_THETA = 10000.0
EPS = 1e-6

V7X_LANES = 128
V7X_SUBLANES = 8
V7X_BF16_SUBLANES = 16
V7X_VMEM_BYTES = 64 * 2**20

FFN_TM, FFN_TF = 1024, 512
INPROJ_TM, INPROJ_TN = 1024, 2048
NORM_ROWS = 256
OUTPROJ_TM = 512
RET_TB = 256
GDN_TB = 256
CONV_ROWS = 32
SMALL_COLS = V7X_LANES


def _vmem_limit(pipelined_bytes, scratch_bytes, temp_bytes):
    need = 2 * pipelined_bytes + scratch_bytes + temp_bytes
    return int(min(need, V7X_VMEM_BYTES * 15 // 16))


def _mm(a, b):
    return jnp.dot(a.astype(BF16), b.astype(BF16), preferred_element_type=F32)


def _mm_nt(a, b):
    return lax.dot_general(a.astype(BF16), b.astype(BF16), (((1,), (1,)), ((), ())),
                           preferred_element_type=F32)


def _mm_tn(a, b):
    return lax.dot_general(a.astype(BF16), b.astype(BF16), (((0,), (0,)), ((), ())),
                           preferred_element_type=F32)


def _rmsnorm(x, w):
    return x * lax.rsqrt(jnp.mean(x * x, axis=-1, keepdims=True) + EPS) * w


def _silu(x):
    return x * jax.nn.sigmoid(x)


def _row_groups(tm):
    return tuple(slice(r, r + NORM_ROWS) for r in range(0, tm, NORM_ROWS))


def _ffn_kernel(x_ref, nw_ref, wg_ref, wu_ref, wd_ref, fw_ref, *rest, final_norm, n_cast, row_groups):
    cast_src = rest[:n_cast]
    o_ref = rest[n_cast]
    cast_dst = rest[n_cast + 1:2 * n_cast + 1]
    h_ref = rest[2 * n_cast + 1]
    j = pl.program_id(1)
    last_j = pl.num_programs(1) - 1

    def tile_term(h):
        g = jnp.dot(h, wg_ref[...], preferred_element_type=F32)
        u = jnp.dot(h, wu_ref[...], preferred_element_type=F32)
        act = (_silu(g) * u) * 0.5
        return jnp.dot(act.astype(BF16), wd_ref[...], preferred_element_type=F32)

    @pl.when(j == 0)
    def _():
        for rows in row_groups:
            x = x_ref[rows, :]
            h = _rmsnorm(x, nw_ref[...]).astype(BF16)
            h_ref[rows, :] = h
            o_ref[rows, :] = x + tile_term(h)

    @pl.when((j > 0) & (j < last_j) if final_norm else (j > 0))
    def _():
        o_ref[...] += tile_term(h_ref[...])

    if final_norm:
        @pl.when(j == last_j)
        def _():
            for rows in row_groups:
                o_ref[rows, :] = _rmsnorm(o_ref[rows, :] + tile_term(h_ref[rows, :]), fw_ref[...])

    for src, dst in zip(cast_src, cast_dst):
        dst[...] = src[...].astype(BF16)


def _cast_plan(w, n_steps):
    rows = w.shape[0]
    slab = V7X_BF16_SUBLANES * pl.cdiv(rows, V7X_BF16_SUBLANES * n_steps)
    assert rows % slab == 0, (w.shape, n_steps)
    return w, slab, rows // slab


def _ffn(x, norm_w, wg, wu, wd, final_w, cast_weights=(), *, final_norm):
    m, d = x.shape
    f = wg.shape[1]
    tm, tf = FFN_TM, FFN_TF
    ni, nj = m // tm, f // tf
    assert nj >= 2, "the first and the last d_ff step are different code paths"
    plans = [_cast_plan(w, ni * nj) for w in cast_weights]
    cast_specs = [pl.BlockSpec((r, v.shape[1]), lambda i, j, used=used: (jnp.minimum(i * nj + j, used - 1), 0))
                  for v, r, used in plans]
    cast_bytes = sum(r * v.shape[1] * (4 + 2) for v, r, _ in plans)
    blocks = 2 * tm * d * 4 + 3 * d * tf * 2 + 2 * d * 4 + cast_bytes
    outs = pl.pallas_call(
        functools.partial(_ffn_kernel, final_norm=final_norm, n_cast=len(plans), row_groups=_row_groups(tm)),
        grid=(ni, nj),
        in_specs=[
            pl.BlockSpec((tm, d), lambda i, j: (i, 0)),
            pl.BlockSpec((1, d), lambda i, j: (0, 0)),
            pl.BlockSpec((d, tf), lambda i, j: (0, j)),
            pl.BlockSpec((d, tf), lambda i, j: (0, j)),
            pl.BlockSpec((tf, d), lambda i, j: (j, 0)),
            pl.BlockSpec((1, d), lambda i, j: (0, 0)),
        ] + cast_specs,
        out_specs=[pl.BlockSpec((tm, d), lambda i, j: (i, 0))] + cast_specs,
        out_shape=[jax.ShapeDtypeStruct((m, d), F32)] + [jax.ShapeDtypeStruct(v.shape, BF16) for v, _, _ in plans],
        scratch_shapes=[pltpu.VMEM((tm, d), BF16)],
        compiler_params=pltpu.CompilerParams(
            dimension_semantics=("arbitrary", "arbitrary"),
            vmem_limit_bytes=_vmem_limit(blocks, tm * d * 2, 4 * tm * tf * 4)),
        name="ffn_final" if final_norm else "ffn",
    )(x, norm_w, wg, wu, wd, final_w, *[v for v, _, _ in plans])
    return outs[0], list(outs[1:])


def _inproj_kernel(x_ref, nw_ref, wt_ref, wst_ref, o_ref, os_ref, h_ref, *, row_groups):
    j = pl.program_id(1)
    project = lambda h, w_ref: lax.dot_general(h, w_ref[...], (((1,), (1,)), ((), ())),
                                               preferred_element_type=F32)

    @pl.when(j == 0)
    def _():
        for rows in row_groups:
            hb = _rmsnorm(x_ref[rows, :], nw_ref[...]).astype(BF16)
            h_ref[rows, :] = hb
            os_ref[rows, :] = project(hb, wst_ref)
            o_ref[rows, :] = project(hb, wt_ref)

    @pl.when(j > 0)
    def _():
        o_ref[...] = project(h_ref[...], wt_ref)


def _inproj(x, norm_w, w_in_t, w_small_t, n_main):
    m, d = x.shape
    tm, tn = INPROJ_TM, INPROJ_TN
    blocks = tm * d * 4 + d * tn * 2 + d * SMALL_COLS * 2 + tm * tn * 4 + tm * SMALL_COLS * 4 + d * 4
    return pl.pallas_call(
        functools.partial(_inproj_kernel, row_groups=_row_groups(tm)),
        grid=(m // tm, n_main // tn),
        in_specs=[
            pl.BlockSpec((tm, d), lambda i, j: (i, 0)),
            pl.BlockSpec((1, d), lambda i, j: (0, 0)),
            pl.BlockSpec((tn, d), lambda i, j: (j, 0)),
            pl.BlockSpec((SMALL_COLS, d), lambda i, j: (0, 0)),
        ],
        out_specs=[
            pl.BlockSpec((tm, tn), lambda i, j: (i, j)),
            pl.BlockSpec((tm, SMALL_COLS), lambda i, j: (i, 0)),
        ],
        out_shape=[
            jax.ShapeDtypeStruct((m, n_main), F32),
            jax.ShapeDtypeStruct((m, SMALL_COLS), F32),
        ],
        scratch_shapes=[pltpu.VMEM((tm, d), BF16)],
        compiler_params=pltpu.CompilerParams(
            dimension_semantics=("parallel", "arbitrary"),
            vmem_limit_bytes=_vmem_limit(blocks, tm * d * 2, tm * d * 4 + tm * tn * 4)),
        name="inproj",
    )(x, norm_w, w_in_t, w_small_t)


def _retention_kernel(pos_ref, invf_ref, sgn_ref, q_ref, k_ref, v_ref, g_ref, o_ref,
                      state_ref, decay_ref, qdec_ref, kdec_ref, *, tb):
    heads = range(RET_HEADS)
    lanes = lambda h: slice(h * HEAD_DIM, (h + 1) * HEAD_DIM)
    log_gamma = [math.log(1.0 - 2.0 ** (-5 - h)) for h in heads]

    @pl.when((pl.program_id(0) == 0) & (pl.program_id(1) == 0))
    def _():
        ri = lax.broadcasted_iota(jnp.int32, (tb, tb), 0)
        ci = lax.broadcasted_iota(jnp.int32, (tb, tb), 1)
        causal = ri >= ci
        rel = jnp.where(causal, (ri - ci).astype(F32), 0.0)
        idx = lax.broadcasted_iota(jnp.int32, (tb, HEAD_DIM), 0).astype(F32)
        for h in heads:
            decay_ref[h] = jnp.where(causal, jnp.exp(log_gamma[h] * rel), 0.0)
            qdec_ref[h] = jnp.exp(log_gamma[h] * (idx + 1.0))
            kdec_ref[h] = jnp.exp(log_gamma[h] * (tb - 1.0 - idx))

    @pl.when(pl.program_id(1) == 0)
    def _():
        state_ref[...] = jnp.zeros_like(state_ref)

    pos = pos_ref[0].astype(F32)
    pos_rows = jnp.broadcast_to(pos, (V7X_LANES, tb)).T
    ang = pos_rows * invf_ref[...]
    cos2 = jnp.cos(ang)
    sin2 = jnp.sin(ang) * sgn_ref[...]
    scale = HEAD_DIM ** -0.5

    rot = lambda x: x * cos2 + pltpu.roll(x, HEAD_DIM // 2, 1) * sin2
    qr = [rot(q_ref[:, lanes(h)]) for h in heads]
    kr = [rot(k_ref[:, lanes(h)]) * scale for h in heads]
    v = [v_ref[:, lanes(h)] for h in heads]
    state = [state_ref[h] for h in heads]
    scores = [_mm_nt(qr[h], kr[h]) for h in heads]
    inter = [_mm(qr[h] * qdec_ref[h], state[h]) for h in heads]
    kv = [_mm_tn(kr[h] * kdec_ref[h], v[h]) for h in heads]
    intra = [_mm(scores[h] * decay_ref[h], v[h]) for h in heads]
    for h in heads:
        state_ref[h] = math.exp(log_gamma[h] * tb) * state[h] + kv[h]
        o = intra[h] + inter[h]
        mu = jnp.mean(o, axis=-1, keepdims=True)
        oc = o - mu
        var = jnp.mean(oc * oc, axis=-1, keepdims=True)
        o_ref[:, lanes(h)] = (oc * lax.rsqrt(var + EPS) * _silu(g_ref[:, lanes(h)])).astype(BF16)


def _retention(proj, pos3, invf2, sgn2, batch, seq):
    tb = RET_TB
    nt = seq // tb
    m = batch * seq
    col = lambda c: pl.BlockSpec((tb, RET_WIDTH), lambda b, t, c=c: (b * nt + t, c))
    blocks = 4 * tb * RET_WIDTH * 4 + tb * RET_WIDTH * 2 + tb * 4 + 2 * V7X_LANES * 4
    scratch = RET_HEADS * (HEAD_DIM * HEAD_DIM + tb * tb + 2 * tb * HEAD_DIM) * 4
    return pl.pallas_call(
        functools.partial(_retention_kernel, tb=tb),
        grid=(batch, nt),
        in_specs=[
            pl.BlockSpec((1, 1, tb), lambda b, t: (b * nt + t, 0, 0)),
            pl.BlockSpec((1, V7X_LANES), lambda b, t: (0, 0)),
            pl.BlockSpec((1, V7X_LANES), lambda b, t: (0, 0)),
            col(0), col(1), col(2), col(3),
        ],
        out_specs=pl.BlockSpec((tb, RET_WIDTH), lambda b, t: (b * nt + t, 0)),
        out_shape=jax.ShapeDtypeStruct((m, RET_WIDTH), BF16),
        scratch_shapes=[
            pltpu.VMEM((RET_HEADS, HEAD_DIM, HEAD_DIM), F32),
            pltpu.VMEM((RET_HEADS, tb, tb), F32),
            pltpu.VMEM((RET_HEADS, tb, HEAD_DIM), F32),
            pltpu.VMEM((RET_HEADS, tb, HEAD_DIM), F32),
        ],
        compiler_params=pltpu.CompilerParams(
            dimension_semantics=("arbitrary", "arbitrary"),
            vmem_limit_bytes=_vmem_limit(blocks, scratch, 16 * tb * tb * 4 + 8 * tb * RET_WIDTH * 4)),
        name="retention",
    )(pos3, invf2, sgn2, proj, proj, proj, proj)


def _softplus(x):
    return jnp.maximum(x, 0.0) + jnp.log1p(jnp.exp(-jnp.abs(x)))


def _unit_lower_inverses(ms, eye, blk16, blk32):
    d = [jnp.where(blk16, m, 0.0) for m in ms]
    d2 = [_mm(a, a) for a in d]
    d4 = [_mm(a, a) for a in d2]
    d8 = [_mm(a, a) for a in d4]
    p = [_mm(eye - a, eye + b) for a, b in zip(d, d2)]
    p = [_mm(a, eye + b) for a, b in zip(p, d4)]
    p = [_mm(a, eye + b) for a, b in zip(p, d8)]
    off16 = blk32 & jnp.logical_not(blk16)
    lp = [_mm(jnp.where(off16, m, 0.0), a) for m, a in zip(ms, p)]
    p = [a - _mm(a, b) for a, b in zip(p, lp)]
    lp = [_mm(jnp.where(blk32, 0.0, m), a) for m, a in zip(ms, p)]
    p = [a - _mm(a, b) for a, b in zip(p, lp)]
    return p


def _gdn_kernel(q_ref, k_ref, v_ref, z_ref, s_ref, cw_ref, alog_ref, dtb_ref, nw_ref,
                o_ref, pad_ref, conv_ref, state_ref, *, tb):
    c_len = GDN_CHUNK
    nc = tb // c_len
    halo = V7X_SUBLANES

    @pl.when(pl.program_id(1) == 0)
    def _():
        state_ref[...] = jnp.zeros_like(state_ref)
        pad_ref[:, 0:halo, :] = jnp.zeros((3, halo, GDN_WIDTH), F32)

    groups = [(0, halo)] + [(lo, min(CONV_ROWS, tb - lo)) for lo in range(halo, tb, CONV_ROWS)]
    for a, ref in enumerate((q_ref, k_ref, v_ref)):
        w = cw_ref[:, a * GDN_WIDTH:(a + 1) * GDN_WIDTH]
        taps = [w[CONV_K - 1 - back:CONV_K - back, :] for back in range(CONV_K)]
        pad_ref[a, halo:2 * halo, :] = ref[0:halo, :]
        for lo, n in groups:
            acc = ref[lo:lo + n, :] * taps[0]
            for back in range(1, CONV_K):
                prev = pad_ref[a, halo - back:2 * halo - back, :] if lo == 0 else ref[lo - back:lo - back + n, :]
                acc = acc + prev * taps[back]
            conv_ref[a, lo:lo + n, :] = _silu(acc)
        pad_ref[a, 0:halo, :] = ref[tb - halo:tb, :]

    small = s_ref[...]
    lane = lax.broadcasted_iota(jnp.int32, (tb, SMALL_COLS), 1)
    g_all = jnp.where(lane < GDN_HEADS, -jnp.exp(alog_ref[...]) * _softplus(small + dtb_ref[...]), 0.0)
    beta_all = jax.nn.sigmoid(small)

    ri = lax.broadcasted_iota(jnp.int32, (tb, tb), 0)
    ci = lax.broadcasted_iota(jnp.int32, (tb, tb), 1)
    chunk_tri = ((ri >= ci) & ((ri >> 6) == (ci >> 6))).astype(F32)
    gc = jnp.dot(chunk_tri, g_all, precision=lax.Precision.HIGHEST, preferred_element_type=F32)
    gct = gc.T

    r64 = lax.broadcasted_iota(jnp.int32, (c_len, c_len), 0)
    c64 = lax.broadcasted_iota(jnp.int32, (c_len, c_len), 1)
    causal = r64 >= c64
    strict = r64 > c64
    eye = (r64 == c64).astype(F32)
    blk16 = (r64 >> 4) == (c64 >> 4)
    blk32 = (r64 >> 5) == (c64 >> 5)
    scale = HEAD_DIM ** -0.5

    heads = range(GDN_HEADS)
    chains = [(h, c) for h in heads for c in range(nc)]
    rows = lambda c: slice(c * c_len, (c + 1) * c_len)
    lanes = lambda h: slice(h * HEAD_DIM, (h + 1) * HEAD_DIM)

    qn, kn, kb, vb, kbe, qe, g_b = [], [], [], [], [], [], []
    for h in heads:
        qh, kh, vh = conv_ref[0, :, lanes(h)], conv_ref[1, :, lanes(h)], conv_ref[2, :, lanes(h)]
        g_h = jnp.broadcast_to(gc[:, h:h + 1], (tb, HEAD_DIM))
        b_h = jnp.broadcast_to(beta_all[:, GDN_HEADS + h:GDN_HEADS + h + 1], (tb, HEAD_DIM))
        e_h = jnp.exp(g_h)
        q_h = qh * lax.rsqrt(jnp.sum(qh * qh, axis=-1, keepdims=True) + EPS) * scale
        k_h = kh * lax.rsqrt(jnp.sum(kh * kh, axis=-1, keepdims=True) + EPS)
        kb_h = k_h * b_h
        qn.append(q_h), kn.append(k_h), kb.append(kb_h), vb.append(vh * b_h)
        kbe.append(kb_h * e_h), qe.append(q_h * e_h), g_b.append(g_h)

    decay = [jnp.exp(jnp.where(causal, g_b[h][rows(c), :c_len] - gct[h:h + 1, rows(c)], -jnp.inf))
             for h, c in chains]
    kq = [_mm_nt(jnp.concatenate([kb[h][rows(c)], qn[h][rows(c)]], axis=0), kn[h][rows(c)])
          for h, c in chains]
    ms = [jnp.where(strict, a[:c_len] * dc, 0.0) for a, dc in zip(kq, decay)]
    attn = [a[c_len:] * dc for a, dc in zip(kq, decay)]
    t_inv = _unit_lower_inverses(ms, eye, blk16, blk32)
    rhs = [jnp.concatenate([vb[h][rows(c)], kbe[h][rows(c)]], axis=1) for h, c in chains]
    sol = [r + _mm(t - eye, r) for t, r in zip(t_inv, rhs)]

    state = [state_ref[h] for h in heads]
    outs = [[] for _ in heads]
    for c in range(nc):
        last = (c + 1) * c_len - 1
        g_last = [g_b[h][last:last + 1, :] for h in heads]
        k_dec = [kn[h][rows(c)] * jnp.exp(g_last[h] - g_b[h][rows(c)]) for h in heads]
        w_s = [_mm(sol[h * nc + c][:, HEAD_DIM:], state[h]) for h in heads]
        q_s = [_mm(qe[h][rows(c)], state[h]) for h in heads]
        v_new = [sol[h * nc + c][:, :HEAD_DIM] - w_s[h] for h in heads]
        a_v = [_mm(attn[h * nc + c], v_new[h]) for h in heads]
        k_v = [_mm_tn(k_dec[h], v_new[h]) for h in heads]
        state = [state[h] * jnp.exp(g_last[h]) + k_v[h] for h in heads]
        for h in heads:
            outs[h].append(q_s[h] + a_v[h])

    for h in heads:
        state_ref[h] = state[h]
        o = jnp.concatenate(outs[h], axis=0)
        o = o * lax.rsqrt(jnp.mean(o * o, axis=-1, keepdims=True) + EPS) * nw_ref[...]
        o_ref[:, lanes(h)] = (o * _silu(z_ref[:, lanes(h)])).astype(BF16)


def _gdn(proj, small, conv_w, alog_row, dtb_row, norm_w, batch, seq):
    tb = GDN_TB
    nt = seq // tb
    m = batch * seq
    first = RET_WIDTH * 4 // GDN_WIDTH
    col = lambda c: pl.BlockSpec((tb, GDN_WIDTH), lambda b, t, c=c: (b * nt + t, first + c))
    row = lambda n: pl.BlockSpec((1, n), lambda b, t: (0, 0))
    blocks = (4 * tb * GDN_WIDTH * 4 + tb * SMALL_COLS * 4 + CONV_K * 3 * GDN_WIDTH * 4
              + tb * GDN_WIDTH * 2 + 3 * V7X_LANES * 4)
    scratch = (3 * 2 * V7X_SUBLANES * GDN_WIDTH * 4 + 3 * tb * GDN_WIDTH * 4
               + GDN_HEADS * HEAD_DIM * HEAD_DIM * 4)
    return pl.pallas_call(
        functools.partial(_gdn_kernel, tb=tb),
        grid=(batch, nt),
        in_specs=[
            col(0), col(1), col(2), col(3),
            pl.BlockSpec((tb, SMALL_COLS), lambda b, t: (b * nt + t, 0)),
            pl.BlockSpec((CONV_K, 3 * GDN_WIDTH), lambda b, t: (0, 0)),
            row(SMALL_COLS), row(SMALL_COLS), row(HEAD_DIM),
        ],
        out_specs=pl.BlockSpec((tb, GDN_WIDTH), lambda b, t: (b * nt + t, 0)),
        out_shape=jax.ShapeDtypeStruct((m, GDN_WIDTH), BF16),
        scratch_shapes=[
            pltpu.VMEM((3, 2 * V7X_SUBLANES, GDN_WIDTH), F32),
            pltpu.VMEM((3, tb, GDN_WIDTH), F32),
            pltpu.VMEM((GDN_HEADS, HEAD_DIM, HEAD_DIM), F32),
        ],
        compiler_params=pltpu.CompilerParams(
            dimension_semantics=("arbitrary", "arbitrary"),
            vmem_limit_bytes=_vmem_limit(blocks, scratch, 12 * tb * GDN_WIDTH * 4)),
        name="gdn",
    )(proj, proj, proj, proj, small, conv_w, alog_row, dtb_row, norm_w)


def _outproj_kernel(x_ref, ro_ref, go_ref, wr_ref, wg_ref, o_ref):
    o_ref[...] = (x_ref[...]
                  + jnp.dot(ro_ref[...], wr_ref[...], preferred_element_type=F32)
                  + jnp.dot(go_ref[...], wg_ref[...], preferred_element_type=F32))


def _outproj(x, ro, go, w_out):
    m, d = x.shape
    tm = OUTPROJ_TM
    blocks = 2 * tm * d * 4 + tm * (RET_WIDTH + GDN_WIDTH) * 2 + (RET_WIDTH + GDN_WIDTH) * d * 2
    return pl.pallas_call(
        _outproj_kernel,
        grid=(m // tm,),
        in_specs=[
            pl.BlockSpec((tm, d), lambda i: (i, 0)),
            pl.BlockSpec((tm, RET_WIDTH), lambda i: (i, 0)),
            pl.BlockSpec((tm, GDN_WIDTH), lambda i: (i, 0)),
            pl.BlockSpec((RET_WIDTH, d), lambda i: (0, 0)),
            pl.BlockSpec((GDN_WIDTH, d), lambda i: (RET_WIDTH // GDN_WIDTH, 0)),
        ],
        out_specs=pl.BlockSpec((tm, d), lambda i: (i, 0)),
        out_shape=jax.ShapeDtypeStruct((m, d), F32),
        compiler_params=pltpu.CompilerParams(
            dimension_semantics=("parallel",),
            vmem_limit_bytes=_vmem_limit(blocks, 0, 2 * tm * d * 4)),
        name="outproj",
    )(x, ro, go, w_out, w_out)


def kernel(x, positions, norm_ffn1_w, ffn1_w_gate, ffn1_w_up, ffn1_w_down, norm_mix_w, w_in, conv_w,
           gdn_a_log, gdn_dt_bias, gdn_norm_w, w_out, norm_ffn2_w, ffn2_w_gate, ffn2_w_up, ffn2_w_down,
           norm_final_w):
    batch, seq, d = x.shape
    depth = norm_ffn1_w.shape[0]
    m = batch * seq
    n_main = 4 * RET_WIDTH + 4 * GDN_WIDTH
    row = lambda v: v.reshape(1, -1).astype(F32)

    half = HEAD_DIM // 2
    inv_freq = ROPE_THETA ** (-jnp.arange(half, dtype=F32) / half)
    invf2 = jnp.concatenate([inv_freq, inv_freq]).reshape(1, HEAD_DIM)
    sgn2 = jnp.concatenate([-jnp.ones((half,), F32), jnp.ones((half,), F32)]).reshape(1, HEAD_DIM)
    pos3 = positions.reshape(m // RET_TB, 1, RET_TB)
    pad_heads = lambda v: jnp.pad(v.astype(F32), (0, SMALL_COLS - GDN_HEADS)).reshape(1, SMALL_COLS)

    xf = x.reshape(m, d)
    for l in range(depth):
        last = l == depth - 1
        later = (jnp.swapaxes(w_in[l], 0, 1), w_out[l], ffn2_w_gate[l], ffn2_w_up[l], ffn2_w_down[l])
        xf, (w_in_b, w_out_b, wg2, wu2, wd2) = _ffn(
            xf, row(norm_ffn1_w[l]), ffn1_w_gate[l].astype(BF16), ffn1_w_up[l].astype(BF16),
            ffn1_w_down[l].astype(BF16), row(norm_final_w), later, final_norm=False)
        w_small = jnp.pad(w_in_b[n_main:], ((0, SMALL_COLS - 2 * GDN_HEADS), (0, 0)))
        proj, small = _inproj(xf, row(norm_mix_w[l]), w_in_b, w_small, n_main)
        ro = _retention(proj, pos3, invf2, sgn2, batch, seq)
        go = _gdn(proj, small, conv_w[l].astype(F32), pad_heads(gdn_a_log[l]), pad_heads(gdn_dt_bias[l]),
                  row(gdn_norm_w[l]), batch, seq)
        xf = _outproj(xf, ro, go, w_out_b)
        xf, _ = _ffn(xf, row(norm_ffn2_w[l]), wg2, wu2, wd2, row(norm_final_w), final_norm=last)
    return xf.reshape(batch, seq, d)
```

```python
import functools
import math

import jax
import jax.numpy as jnp
from jax import lax
from jax.experimental import pallas as pl
from jax.experimental.pallas import tpu as pltpu

F32 = jnp.float32
BF16 = jnp.bfloat16

D_MODEL = 2048
HEAD_DIM = 128
RET_HEADS = 8
GDN_HEADS = 8
RET_WIDTH = RET_HEADS * HEAD_DIM
GDN_WIDTH = GDN_HEADS * HEAD_DIM
D_FF = 5632
GDN_CHUNK = 64
CONV_K = 4
ROPE_THETA = 10000.0
EPS = 1e-6

V7X_LANES = 128
V7X_SUBLANES = 8
V7X_BF16_SUBLANES = 16
V7X_VMEM_BYTES = 64 * 2**20

FFN_TM, FFN_TF = 1024, 512
INPROJ_TM, INPROJ_TN = 1024, 2048
OUTPROJ_TM = 512
RET_TB = 256
GDN_TB = 256
CONV_ROWS = 32
SMALL_COLS = V7X_LANES


def _vmem_limit(pipelined_bytes, scratch_bytes, temp_bytes):
    need = 2 * pipelined_bytes + scratch_bytes + temp_bytes
    return int(min(need, V7X_VMEM_BYTES * 15 // 16))


def _mm(a, b):
    return jnp.dot(a.astype(BF16), b.astype(BF16), preferred_element_type=F32)


def _mm_nt(a, b):
    return lax.dot_general(a.astype(BF16), b.astype(BF16), (((1,), (1,)), ((), ())),
                           preferred_element_type=F32)


def _mm_tn(a, b):
    return lax.dot_general(a.astype(BF16), b.astype(BF16), (((0,), (0,)), ((), ())),
                           preferred_element_type=F32)


def _rmsnorm(x, w):
    return x * lax.rsqrt(jnp.mean(x * x, axis=-1, keepdims=True) + EPS) * w


def _silu(x):
    return x * jax.nn.sigmoid(x)


def _ffn_kernel(x_ref, nw_ref, wg_ref, wu_ref, wd_ref, fw_ref, *rest, final_norm, cast_tiles):
    n_cast = len(cast_tiles)
    cast_src = rest[:n_cast]
    o_ref = rest[n_cast]
    cast_dst = rest[n_cast + 1:2 * n_cast + 1]
    h_ref = rest[2 * n_cast + 1]
    j = pl.program_id(1)

    @pl.when(j == 0)
    def _():
        x = x_ref[...]
        h_ref[...] = _rmsnorm(x, nw_ref[...]).astype(BF16)
        o_ref[...] = x

    h = h_ref[...]
    g = jnp.dot(h, wg_ref[0], preferred_element_type=F32)
    u = jnp.dot(h, wu_ref[0], preferred_element_type=F32)
    act = (_silu(g) * u) * 0.5
    o_ref[...] += jnp.dot(act.astype(BF16), wd_ref[...], preferred_element_type=F32)
    for src, dst, tile in zip(cast_src, cast_dst, cast_tiles):
        if tile:
            for c in range(src.shape[1] // tile):
                dst[c] = src[:, c * tile:(c + 1) * tile].astype(BF16)
        else:
            dst[...] = src[...].astype(BF16)

    if final_norm:
        @pl.when(j == pl.num_programs(1) - 1)
        def _():
            o_ref[...] = _rmsnorm(o_ref[...], fw_ref[...])


def _cast_plan(w, n_steps):
    rows = w.shape[0]
    slab = V7X_BF16_SUBLANES * pl.cdiv(rows, V7X_BF16_SUBLANES * n_steps)
    assert rows % slab == 0, (w.shape, n_steps)
    return w, slab, rows // slab


def _ffn(x, norm_w, wg, wu, wd, final_w, cast_weights=(), *, final_norm):
    m, d = x.shape
    tm, tf = FFN_TM, FFN_TF
    ni, nj = m // tm, wg.shape[0]
    assert wg.shape == wu.shape == (nj, d, tf) and wd.shape == (nj * tf, d)
    plans = [_cast_plan(w, ni * nj) for w, _ in cast_weights]
    step = lambda used: (lambda i, j: jnp.minimum(i * nj + j, used - 1))
    src_specs, dst_specs, dst_shapes, cast_tiles = [], [], [], []
    for (w, slab, used), (_, tiled) in zip(plans, cast_weights):
        rows, cols = w.shape
        src_specs.append(pl.BlockSpec((slab, cols), lambda i, j, at=step(used): (at(i, j), 0)))
        if tiled:
            dst_specs.append(pl.BlockSpec((cols // tf, slab, tf), lambda i, j, at=step(used): (0, at(i, j), 0)))
            dst_shapes.append(jax.ShapeDtypeStruct((cols // tf, rows, tf), BF16))
        else:
            dst_specs.append(pl.BlockSpec((slab, cols), lambda i, j, at=step(used): (at(i, j), 0)))
            dst_shapes.append(jax.ShapeDtypeStruct((rows, cols), BF16))
        cast_tiles.append(tf if tiled else 0)
    cast_bytes = sum(slab * w.shape[1] * (4 + 2) for w, slab, _ in plans)
    blocks = 2 * tm * d * 4 + 3 * d * tf * 2 + 2 * d * 4 + cast_bytes
    outs = pl.pallas_call(
        functools.partial(_ffn_kernel, final_norm=final_norm, cast_tiles=tuple(cast_tiles)),
        grid=(ni, nj),
        in_specs=[
            pl.BlockSpec((tm, d), lambda i, j: (i, 0)),
            pl.BlockSpec((1, d), lambda i, j: (0, 0)),
            pl.BlockSpec((1, d, tf), lambda i, j: (j, 0, 0)),
            pl.BlockSpec((1, d, tf), lambda i, j: (j, 0, 0)),
            pl.BlockSpec((tf, d), lambda i, j: (j, 0)),
            pl.BlockSpec((1, d), lambda i, j: (0, 0)),
        ] + src_specs,
        out_specs=[pl.BlockSpec((tm, d), lambda i, j: (i, 0))] + dst_specs,
        out_shape=[jax.ShapeDtypeStruct((m, d), F32)] + dst_shapes,
        scratch_shapes=[pltpu.VMEM((tm, d), BF16)],
        compiler_params=pltpu.CompilerParams(
            dimension_semantics=("arbitrary", "arbitrary"),
            vmem_limit_bytes=_vmem_limit(blocks, tm * d * 2, 4 * tm * tf * 4)),
        name="ffn_final" if final_norm else "ffn",
    )(x, norm_w, wg, wu, wd, final_w, *[w for w, _, _ in plans])
    return outs[0], list(outs[1:])


def _inproj_kernel(x_ref, nw_ref, wt_ref, wst_ref, o_ref, os_ref, h_ref):
    j = pl.program_id(1)

    @pl.when(j == 0)
    def _():
        hb = _rmsnorm(x_ref[...], nw_ref[...]).astype(BF16)
        h_ref[...] = hb
        os_ref[...] = lax.dot_general(hb, wst_ref[...], (((1,), (1,)), ((), ())), preferred_element_type=F32)

    o_ref[...] = lax.dot_general(h_ref[...], wt_ref[...], (((1,), (1,)), ((), ())), preferred_element_type=F32)


def _inproj(x, norm_w, w_in_t, w_small_t, n_main):
    m, d = x.shape
    tm, tn = INPROJ_TM, INPROJ_TN
    blocks = tm * d * 4 + d * tn * 2 + d * SMALL_COLS * 2 + tm * tn * 4 + tm * SMALL_COLS * 4 + d * 4
    return pl.pallas_call(
        _inproj_kernel,
        grid=(m // tm, n_main // tn),
        in_specs=[
            pl.BlockSpec((tm, d), lambda i, j: (i, 0)),
            pl.BlockSpec((1, d), lambda i, j: (0, 0)),
            pl.BlockSpec((tn, d), lambda i, j: (j, 0)),
            pl.BlockSpec((SMALL_COLS, d), lambda i, j: (0, 0)),
        ],
        out_specs=[
            pl.BlockSpec((tm, tn), lambda i, j: (i, j)),
            pl.BlockSpec((tm, SMALL_COLS), lambda i, j: (i, 0)),
        ],
        out_shape=[
            jax.ShapeDtypeStruct((m, n_main), F32),
            jax.ShapeDtypeStruct((m, SMALL_COLS), F32),
        ],
        scratch_shapes=[pltpu.VMEM((tm, d), BF16)],
        compiler_params=pltpu.CompilerParams(
            dimension_semantics=("parallel", "arbitrary"),
            vmem_limit_bytes=_vmem_limit(blocks, tm * d * 2, tm * d * 4 + tm * tn * 4)),
        name="inproj",
    )(x, norm_w, w_in_t, w_small_t)


def _retention_kernel(pos_ref, invf_ref, sgn_ref, q_ref, k_ref, v_ref, g_ref, o_ref,
                      state_ref, decay_ref, qdec_ref, kdec_ref, *, tb):
    heads = range(RET_HEADS)
    lanes = lambda h: slice(h * HEAD_DIM, (h + 1) * HEAD_DIM)
    log_gamma = [math.log(1.0 - 2.0 ** (-5 - h)) for h in heads]

    @pl.when((pl.program_id(0) == 0) & (pl.program_id(1) == 0))
    def _():
        ri = lax.broadcasted_iota(jnp.int32, (tb, tb), 0)
        ci = lax.broadcasted_iota(jnp.int32, (tb, tb), 1)
        causal = ri >= ci
        rel = jnp.where(causal, (ri - ci).astype(F32), 0.0)
        idx = lax.broadcasted_iota(jnp.int32, (tb, HEAD_DIM), 0).astype(F32)
        for h in heads:
            decay_ref[h] = jnp.where(causal, jnp.exp(log_gamma[h] * rel), 0.0)
            qdec_ref[h] = jnp.exp(log_gamma[h] * (idx + 1.0))
            kdec_ref[h] = jnp.exp(log_gamma[h] * (tb - 1.0 - idx))

    @pl.when(pl.program_id(1) == 0)
    def _():
        state_ref[...] = jnp.zeros_like(state_ref)

    pos = pos_ref[0].astype(F32)
    pos_rows = jnp.broadcast_to(pos, (V7X_LANES, tb)).T
    ang = pos_rows * invf_ref[...]
    cos2 = jnp.cos(ang)
    sin2 = jnp.sin(ang) * sgn_ref[...]
    scale = HEAD_DIM ** -0.5

    rot = lambda x: x * cos2 + pltpu.roll(x, HEAD_DIM // 2, 1) * sin2
    qr = [rot(q_ref[:, lanes(h)]) for h in heads]
    kr = [rot(k_ref[:, lanes(h)]) * scale for h in heads]
    v = [v_ref[:, lanes(h)] for h in heads]
    state = [state_ref[h] for h in heads]
    scores = [_mm_nt(qr[h], kr[h]) for h in heads]
    inter = [_mm(qr[h] * qdec_ref[h], state[h]) for h in heads]
    kv = [_mm_tn(kr[h] * kdec_ref[h], v[h]) for h in heads]
    intra = [_mm(scores[h] * decay_ref[h], v[h]) for h in heads]
    for h in heads:
        state_ref[h] = math.exp(log_gamma[h] * tb) * state[h] + kv[h]
        o = intra[h] + inter[h]
        mu = jnp.mean(o, axis=-1, keepdims=True)
        oc = o - mu
        var = jnp.mean(oc * oc, axis=-1, keepdims=True)
        o_ref[:, lanes(h)] = (oc * lax.rsqrt(var + EPS) * _silu(g_ref[:, lanes(h)])).astype(BF16)


def _retention(proj, pos3, invf2, sgn2, batch, seq):
    tb = RET_TB
    nt = seq // tb
    m = batch * seq
    col = lambda c: pl.BlockSpec((tb, RET_WIDTH), lambda b, t, c=c: (b * nt + t, c))
    blocks = 4 * tb * RET_WIDTH * 4 + tb * RET_WIDTH * 2 + tb * 4 + 2 * V7X_LANES * 4
    scratch = RET_HEADS * (HEAD_DIM * HEAD_DIM + tb * tb + 2 * tb * HEAD_DIM) * 4
    return pl.pallas_call(
        functools.partial(_retention_kernel, tb=tb),
        grid=(batch, nt),
        in_specs=[
            pl.BlockSpec((1, 1, tb), lambda b, t: (b * nt + t, 0, 0)),
            pl.BlockSpec((1, V7X_LANES), lambda b, t: (0, 0)),
            pl.BlockSpec((1, V7X_LANES), lambda b, t: (0, 0)),
            col(0), col(1), col(2), col(3),
        ],
        out_specs=pl.BlockSpec((tb, RET_WIDTH), lambda b, t: (b * nt + t, 0)),
        out_shape=jax.ShapeDtypeStruct((m, RET_WIDTH), BF16),
        scratch_shapes=[
            pltpu.VMEM((RET_HEADS, HEAD_DIM, HEAD_DIM), F32),
            pltpu.VMEM((RET_HEADS, tb, tb), F32),
            pltpu.VMEM((RET_HEADS, tb, HEAD_DIM), F32),
            pltpu.VMEM((RET_HEADS, tb, HEAD_DIM), F32),
        ],
        compiler_params=pltpu.CompilerParams(
            dimension_semantics=("arbitrary", "arbitrary"),
            vmem_limit_bytes=_vmem_limit(blocks, scratch, 16 * tb * tb * 4 + 8 * tb * RET_WIDTH * 4)),
        name="retention",
    )(pos3, invf2, sgn2, proj, proj, proj, proj)


def _softplus(x):
    return jnp.maximum(x, 0.0) + jnp.log1p(jnp.exp(-jnp.abs(x)))


def _unit_lower_inverses(ms, eye, blk16, blk32):
    d = [jnp.where(blk16, m, 0.0) for m in ms]
    d2 = [_mm(a, a) for a in d]
    d4 = [_mm(a, a) for a in d2]
    d8 = [_mm(a, a) for a in d4]
    p = [_mm(eye - a, eye + b) for a, b in zip(d, d2)]
    p = [_mm(a, eye + b) for a, b in zip(p, d4)]
    p = [_mm(a, eye + b) for a, b in zip(p, d8)]
    off16 = blk32 & jnp.logical_not(blk16)
    lp = [_mm(jnp.where(off16, m, 0.0), a) for m, a in zip(ms, p)]
    p = [a - _mm(a, b) for a, b in zip(p, lp)]
    lp = [_mm(jnp.where(blk32, 0.0, m), a) for m, a in zip(ms, p)]
    p = [a - _mm(a, b) for a, b in zip(p, lp)]
    return p


def _gdn_kernel(q_ref, k_ref, v_ref, z_ref, s_ref, cw_ref, alog_ref, dtb_ref, nw_ref,
                o_ref, pad_ref, conv_ref, state_ref, *, tb):
    c_len = GDN_CHUNK
    nc = tb // c_len
    halo = V7X_SUBLANES

    @pl.when(pl.program_id(1) == 0)
    def _():
        state_ref[...] = jnp.zeros_like(state_ref)
        pad_ref[:, 0:halo, :] = jnp.zeros((3, halo, GDN_WIDTH), F32)

    groups = [(0, halo)] + [(lo, min(CONV_ROWS, tb - lo)) for lo in range(halo, tb, CONV_ROWS)]
    for a, ref in enumerate((q_ref, k_ref, v_ref)):
        w = cw_ref[:, a * GDN_WIDTH:(a + 1) * GDN_WIDTH]
        taps = [w[CONV_K - 1 - back:CONV_K - back, :] for back in range(CONV_K)]
        pad_ref[a, halo:2 * halo, :] = ref[0:halo, :]
        for lo, n in groups:
            acc = ref[lo:lo + n, :] * taps[0]
            for back in range(1, CONV_K):
                prev = pad_ref[a, halo - back:2 * halo - back, :] if lo == 0 else ref[lo - back:lo - back + n, :]
                acc = acc + prev * taps[back]
            conv_ref[a, lo:lo + n, :] = _silu(acc)
        pad_ref[a, 0:halo, :] = ref[tb - halo:tb, :]

    small = s_ref[...]
    lane = lax.broadcasted_iota(jnp.int32, (tb, SMALL_COLS), 1)
    g_all = jnp.where(lane < GDN_HEADS, -jnp.exp(alog_ref[...]) * _softplus(small + dtb_ref[...]), 0.0)
    beta_all = jax.nn.sigmoid(small)

    ri = lax.broadcasted_iota(jnp.int32, (tb, tb), 0)
    ci = lax.broadcasted_iota(jnp.int32, (tb, tb), 1)
    chunk_tri = ((ri >= ci) & ((ri >> 6) == (ci >> 6))).astype(F32)
    gc = jnp.dot(chunk_tri, g_all, precision=lax.Precision.HIGHEST, preferred_element_type=F32)
    gct = gc.T

    r64 = lax.broadcasted_iota(jnp.int32, (c_len, c_len), 0)
    c64 = lax.broadcasted_iota(jnp.int32, (c_len, c_len), 1)
    causal = r64 >= c64
    strict = r64 > c64
    eye = (r64 == c64).astype(F32)
    blk16 = (r64 >> 4) == (c64 >> 4)
    blk32 = (r64 >> 5) == (c64 >> 5)
    scale = HEAD_DIM ** -0.5

    heads = range(GDN_HEADS)
    chains = [(h, c) for h in heads for c in range(nc)]
    rows = lambda c: slice(c * c_len, (c + 1) * c_len)
    lanes = lambda h: slice(h * HEAD_DIM, (h + 1) * HEAD_DIM)

    qn, kn, kb, vb, kbe, qe, g_b = [], [], [], [], [], [], []
    for h in heads:
        qh, kh, vh = conv_ref[0, :, lanes(h)], conv_ref[1, :, lanes(h)], conv_ref[2, :, lanes(h)]
        g_h = jnp.broadcast_to(gc[:, h:h + 1], (tb, HEAD_DIM))
        b_h = jnp.broadcast_to(beta_all[:, GDN_HEADS + h:GDN_HEADS + h + 1], (tb, HEAD_DIM))
        e_h = jnp.exp(g_h)
        q_h = qh * lax.rsqrt(jnp.sum(qh * qh, axis=-1, keepdims=True) + EPS) * scale
        k_h = kh * lax.rsqrt(jnp.sum(kh * kh, axis=-1, keepdims=True) + EPS)
        kb_h = k_h * b_h
        qn.append(q_h), kn.append(k_h), kb.append(kb_h), vb.append(vh * b_h)
        kbe.append(kb_h * e_h), qe.append(q_h * e_h), g_b.append(g_h)

    decay = [jnp.exp(jnp.where(causal, g_b[h][rows(c), :c_len] - gct[h:h + 1, rows(c)], -jnp.inf))
             for h, c in chains]
    kq = [_mm_nt(jnp.concatenate([kb[h][rows(c)], qn[h][rows(c)]], axis=0), kn[h][rows(c)])
          for h, c in chains]
    ms = [jnp.where(strict, a[:c_len] * dc, 0.0) for a, dc in zip(kq, decay)]
    attn = [a[c_len:] * dc for a, dc in zip(kq, decay)]
    t_inv = _unit_lower_inverses(ms, eye, blk16, blk32)
    rhs = [jnp.concatenate([vb[h][rows(c)], kbe[h][rows(c)]], axis=1) for h, c in chains]
    sol = [r + _mm(t - eye, r) for t, r in zip(t_inv, rhs)]

    state = [state_ref[h] for h in heads]
    outs = [[] for _ in heads]
    for c in range(nc):
        last = (c + 1) * c_len - 1
        g_last = [g_b[h][last:last + 1, :] for h in heads]
        k_dec = [kn[h][rows(c)] * jnp.exp(g_last[h] - g_b[h][rows(c)]) for h in heads]
        w_s = [_mm(sol[h * nc + c][:, HEAD_DIM:], state[h]) for h in heads]
        q_s = [_mm(qe[h][rows(c)], state[h]) for h in heads]
        v_new = [sol[h * nc + c][:, :HEAD_DIM] - w_s[h] for h in heads]
        a_v = [_mm(attn[h * nc + c], v_new[h]) for h in heads]
        k_v = [_mm_tn(k_dec[h], v_new[h]) for h in heads]
        state = [state[h] * jnp.exp(g_last[h]) + k_v[h] for h in heads]
        for h in heads:
            outs[h].append(q_s[h] + a_v[h])

    for h in heads:
        state_ref[h] = state[h]
        o = jnp.concatenate(outs[h], axis=0)
        o = o * lax.rsqrt(jnp.mean(o * o, axis=-1, keepdims=True) + EPS) * nw_ref[...]
        o_ref[:, lanes(h)] = (o * _silu(z_ref[:, lanes(h)])).astype(BF16)


def _gdn(proj, small, conv_w, alog_row, dtb_row, norm_w, batch, seq):
    tb = GDN_TB
    nt = seq // tb
    m = batch * seq
    first = RET_WIDTH * 4 // GDN_WIDTH
    col = lambda c: pl.BlockSpec((tb, GDN_WIDTH), lambda b, t, c=c: (b * nt + t, first + c))
    row = lambda n: pl.BlockSpec((1, n), lambda b, t: (0, 0))
    blocks = (4 * tb * GDN_WIDTH * 4 + tb * SMALL_COLS * 4 + CONV_K * 3 * GDN_WIDTH * 4
              + tb * GDN_WIDTH * 2 + 3 * V7X_LANES * 4)
    scratch = (3 * 2 * V7X_SUBLANES * GDN_WIDTH * 4 + 3 * tb * GDN_WIDTH * 4
               + GDN_HEADS * HEAD_DIM * HEAD_DIM * 4)
    return pl.pallas_call(
        functools.partial(_gdn_kernel, tb=tb),
        grid=(batch, nt),
        in_specs=[
            col(0), col(1), col(2), col(3),
            pl.BlockSpec((tb, SMALL_COLS), lambda b, t: (b * nt + t, 0)),
            pl.BlockSpec((CONV_K, 3 * GDN_WIDTH), lambda b, t: (0, 0)),
            row(SMALL_COLS), row(SMALL_COLS), row(HEAD_DIM),
        ],
        out_specs=pl.BlockSpec((tb, GDN_WIDTH), lambda b, t: (b * nt + t, 0)),
        out_shape=jax.ShapeDtypeStruct((m, GDN_WIDTH), BF16),
        scratch_shapes=[
            pltpu.VMEM((3, 2 * V7X_SUBLANES, GDN_WIDTH), F32),
            pltpu.VMEM((3, tb, GDN_WIDTH), F32),
            pltpu.VMEM((GDN_HEADS, HEAD_DIM, HEAD_DIM), F32),
        ],
        compiler_params=pltpu.CompilerParams(
            dimension_semantics=("arbitrary", "arbitrary"),
            vmem_limit_bytes=_vmem_limit(blocks, scratch, 12 * tb * GDN_WIDTH * 4)),
        name="gdn",
    )(proj, proj, proj, proj, small, conv_w, alog_row, dtb_row, norm_w)


def _outproj_kernel(x_ref, ro_ref, go_ref, wr_ref, wg_ref, o_ref):
    o_ref[...] = (x_ref[...]
                  + jnp.dot(ro_ref[...], wr_ref[...], preferred_element_type=F32)
                  + jnp.dot(go_ref[...], wg_ref[...], preferred_element_type=F32))


def _outproj(x, ro, go, w_out):
    m, d = x.shape
    tm = OUTPROJ_TM
    blocks = 2 * tm * d * 4 + tm * (RET_WIDTH + GDN_WIDTH) * 2 + (RET_WIDTH + GDN_WIDTH) * d * 2
    return pl.pallas_call(
        _outproj_kernel,
        grid=(m // tm,),
        in_specs=[
            pl.BlockSpec((tm, d), lambda i: (i, 0)),
            pl.BlockSpec((tm, RET_WIDTH), lambda i: (i, 0)),
            pl.BlockSpec((tm, GDN_WIDTH), lambda i: (i, 0)),
            pl.BlockSpec((RET_WIDTH, d), lambda i: (0, 0)),
            pl.BlockSpec((GDN_WIDTH, d), lambda i: (RET_WIDTH // GDN_WIDTH, 0)),
        ],
        out_specs=pl.BlockSpec((tm, d), lambda i: (i, 0)),
        out_shape=jax.ShapeDtypeStruct((m, d), F32),
        compiler_params=pltpu.CompilerParams(
            dimension_semantics=("parallel",),
            vmem_limit_bytes=_vmem_limit(blocks, 0, 2 * tm * d * 4)),
        name="outproj",
    )(x, ro, go, w_out, w_out)


def kernel(x, positions, norm_ffn1_w, ffn1_w_gate, ffn1_w_up, ffn1_w_down, norm_mix_w, w_in, conv_w,
           gdn_a_log, gdn_dt_bias, gdn_norm_w, w_out, norm_ffn2_w, ffn2_w_gate, ffn2_w_up, ffn2_w_down,
           norm_final_w):
    batch, seq, d = x.shape
    depth = norm_ffn1_w.shape[0]
    m = batch * seq
    n_main = 4 * RET_WIDTH + 4 * GDN_WIDTH
    row = lambda v: v.reshape(1, -1).astype(F32)

    half = HEAD_DIM // 2
    inv_freq = ROPE_THETA ** (-jnp.arange(half, dtype=F32) / half)
    invf2 = jnp.concatenate([inv_freq, inv_freq]).reshape(1, HEAD_DIM)
    sgn2 = jnp.concatenate([-jnp.ones((half,), F32), jnp.ones((half,), F32)]).reshape(1, HEAD_DIM)
    pos3 = positions.reshape(m // RET_TB, 1, RET_TB)
    pad_heads = lambda v: jnp.pad(v.astype(F32), (0, SMALL_COLS - GDN_HEADS)).reshape(1, SMALL_COLS)

    xf = x.reshape(m, d)
    for l in range(depth):
        last = l == depth - 1
        later = ((jnp.swapaxes(w_in[l], 0, 1), False), (w_out[l], False),
                 (ffn2_w_gate[l], True), (ffn2_w_up[l], True), (ffn2_w_down[l], False))
        tiled = lambda w: w.reshape(d, -1, FFN_TF).swapaxes(0, 1).astype(BF16)
        xf, (w_in_b, w_out_b, wg2, wu2, wd2) = _ffn(
            xf, row(norm_ffn1_w[l]), tiled(ffn1_w_gate[l]), tiled(ffn1_w_up[l]),
            ffn1_w_down[l].astype(BF16), row(norm_final_w), later, final_norm=False)
        w_small = jnp.pad(w_in_b[n_main:], ((0, SMALL_COLS - 2 * GDN_HEADS), (0, 0)))
        proj, small = _inproj(xf, row(norm_mix_w[l]), w_in_b, w_small, n_main)
        ro = _retention(proj, pos3, invf2, sgn2, batch, seq)
        go = _gdn(proj, small, conv_w[l].astype(F32), pad_heads(gdn_a_log[l]), pad_heads(gdn_dt_bias[l]),
                  row(gdn_norm_w[l]), batch, seq)
        xf = _outproj(xf, ro, go, w_out_b)
        xf, _ = _ffn(xf, row(norm_ffn2_w[l]), wg2, wu2, wd2, row(norm_final_w), final_norm=last)
    return xf.reshape(batch, seq, d)
```

```python
import functools
import math

import jax
import jax.numpy as jnp
from jax import lax
from jax.experimental import pallas as pl
from jax.experimental.pallas import tpu as pltpu

F32 = jnp.float32
BF16 = jnp.bfloat16

D_MODEL = 2048
HEAD_DIM = 128
RET_HEADS = 8
GDN_HEADS = 8
RET_WIDTH = RET_HEADS * HEAD_DIM
GDN_WIDTH = GDN_HEADS * HEAD_DIM
D_FF = 5632
GDN_CHUNK = 64
CONV_K = 4
ROPE_THETA = 10000.0
EPS = 1e-6

V7X_LANES = 128
V7X_SUBLANES = 8
V7X_BF16_SUBLANES = 16
V7X_VMEM_BYTES = 64 * 2**20

FFN_TM, FFN_TF = 1024, 512
FFN_HEAD_TF = 256
INPROJ_TM, INPROJ_TN = 1024, 2048
OUTPROJ_TM = 512
RET_TB = 256
GDN_TB = 256
CONV_ROWS = 32
SMALL_COLS = V7X_LANES


def _vmem_limit(pipelined_bytes, scratch_bytes, temp_bytes):
    need = 2 * pipelined_bytes + scratch_bytes + temp_bytes
    return int(min(need, V7X_VMEM_BYTES * 15 // 16))


def _mm(a, b):
    return jnp.dot(a.astype(BF16), b.astype(BF16), preferred_element_type=F32)


def _mm_nt(a, b):
    return lax.dot_general(a.astype(BF16), b.astype(BF16), (((1,), (1,)), ((), ())),
                           preferred_element_type=F32)


def _mm_tn(a, b):
    return lax.dot_general(a.astype(BF16), b.astype(BF16), (((0,), (0,)), ((), ())),
                           preferred_element_type=F32)


def _rmsnorm(x, w):
    return x * lax.rsqrt(jnp.mean(x * x, axis=-1, keepdims=True) + EPS) * w


def _silu(x):
    return x * jax.nn.sigmoid(x)


def _ffn_kernel(x_ref, nw_ref, wg_ref, wu_ref, wd_ref, fw_ref, *rest, final_norm, n_cast):
    cast_src = rest[:n_cast]
    o_ref = rest[n_cast]
    cast_dst = rest[n_cast + 1:2 * n_cast + 1]
    h_ref = rest[2 * n_cast + 1]
    j = pl.program_id(1)

    @pl.when(j == 0)
    def _():
        x = x_ref[...]
        h_ref[...] = _rmsnorm(x, nw_ref[...]).astype(BF16)
        o_ref[...] = x

    h = h_ref[...]
    g = jnp.dot(h, wg_ref[...], preferred_element_type=F32)
    u = jnp.dot(h, wu_ref[...], preferred_element_type=F32)
    act = (_silu(g) * u) * 0.5
    o_ref[...] += jnp.dot(act.astype(BF16), wd_ref[...], preferred_element_type=F32)
    for src, dst in zip(cast_src, cast_dst):
        dst[...] = src[...].astype(BF16)

    if final_norm:
        @pl.when(j == pl.num_programs(1) - 1)
        def _():
            o_ref[...] = _rmsnorm(o_ref[...], fw_ref[...])


def _cast_plan(w, rows, n_steps):
    slab = V7X_BF16_SUBLANES * pl.cdiv(rows, V7X_BF16_SUBLANES * n_steps)
    while rows % slab:
        slab += V7X_BF16_SUBLANES
    return w, slab, rows // slab


def _ffn(x, norm_w, wg, wu, wd, final_w, cast_weights=(), *, final_norm, done=None):
    m, d = x.shape
    f = wg.shape[1]
    tm, tf = FFN_TM, FFN_TF
    skip = 0 if done is None else done[1]
    ni, nj = m // tm - skip, f // tf
    plans = [_cast_plan(w, rows, ni * nj) for w, rows in cast_weights]
    cast_specs = [pl.BlockSpec((r, v.shape[1]), lambda i, j, used=used: (jnp.minimum(i * nj + j, used - 1), 0))
                  for v, r, used in plans]
    cast_bytes = sum(r * v.shape[1] * (4 + 2) for v, r, _ in plans)
    blocks = 2 * tm * d * 4 + 3 * d * tf * 2 + 2 * d * 4 + cast_bytes
    operands = [x, norm_w, wg, wu, wd, final_w] + [v for v, _, _ in plans]
    in_specs = [
        pl.BlockSpec((tm, d), lambda i, j: (i + skip, 0)),
        pl.BlockSpec((1, d), lambda i, j: (0, 0)),
        pl.BlockSpec((d, tf), lambda i, j: (0, j)),
        pl.BlockSpec((d, tf), lambda i, j: (0, j)),
        pl.BlockSpec((tf, d), lambda i, j: (j, 0)),
        pl.BlockSpec((1, d), lambda i, j: (0, 0)),
    ] + cast_specs
    body = functools.partial(_ffn_kernel, final_norm=final_norm, n_cast=len(plans))
    aliases = {}
    if done is not None:
        operands.append(done[0])
        in_specs.append(pl.BlockSpec(memory_space=pl.ANY))
        aliases = {len(operands) - 1: 0}
        body = functools.partial(_drop_operand, body, len(operands) - 1)
    outs = pl.pallas_call(
        body,
        grid=(ni, nj),
        in_specs=in_specs,
        out_specs=[pl.BlockSpec((tm, d), lambda i, j: (i + skip, 0))] + cast_specs,
        out_shape=[jax.ShapeDtypeStruct((m, d), F32)]
        + [jax.ShapeDtypeStruct((used * r, v.shape[1]), BF16) for v, r, used in plans],
        scratch_shapes=[pltpu.VMEM((tm, d), BF16)],
        input_output_aliases=aliases,
        compiler_params=pltpu.CompilerParams(
            dimension_semantics=("arbitrary", "arbitrary"),
            vmem_limit_bytes=_vmem_limit(blocks, tm * d * 2, 4 * tm * tf * 4)),
        name="ffn_final" if final_norm else "ffn",
    )(*operands)
    return outs[0], list(outs[1:])


def _drop_operand(body, index, *refs, **kwargs):
    return body(*refs[:index], *refs[index + 1:], **kwargs)


def _ffn_head_kernel(x_ref, nw_ref, wg_ref, wu_ref, wd_ref, o_ref, wgb_ref, wub_ref, wdb_ref, h_ref):
    j = pl.program_id(0)

    @pl.when(j == 0)
    def _():
        x = x_ref[...]
        h_ref[...] = _rmsnorm(x, nw_ref[...]).astype(BF16)
        o_ref[...] = x

    wg = wg_ref[...].astype(BF16)
    wu = wu_ref[...].astype(BF16)
    wd = wd_ref[...].astype(BF16)
    wgb_ref[...] = wg
    wub_ref[...] = wu
    wdb_ref[...] = wd
    h = h_ref[...]
    g = jnp.dot(h, wg, preferred_element_type=F32)
    u = jnp.dot(h, wu, preferred_element_type=F32)
    act = (_silu(g) * u) * 0.5
    o_ref[...] += jnp.dot(act.astype(BF16), wd, preferred_element_type=F32)


def _ffn_head(x, norm_w, wg, wu, wd):
    m, d = x.shape
    f = wg.shape[1]
    tm, tf = FFN_TM, FFN_HEAD_TF
    blocks = 2 * tm * d * 4 + 3 * d * tf * (4 + 2) + d * 4
    return pl.pallas_call(
        _ffn_head_kernel,
        grid=(f // tf,),
        in_specs=[
            pl.BlockSpec((tm, d), lambda j: (0, 0)),
            pl.BlockSpec((1, d), lambda j: (0, 0)),
            pl.BlockSpec((d, tf), lambda j: (0, j)),
            pl.BlockSpec((d, tf), lambda j: (0, j)),
            pl.BlockSpec((tf, d), lambda j: (j, 0)),
        ],
        out_specs=[
            pl.BlockSpec((tm, d), lambda j: (0, 0)),
            pl.BlockSpec((d, tf), lambda j: (0, j)),
            pl.BlockSpec((d, tf), lambda j: (0, j)),
            pl.BlockSpec((tf, d), lambda j: (j, 0)),
        ],
        out_shape=[
            jax.ShapeDtypeStruct((m, d), F32),
            jax.ShapeDtypeStruct((d, f), BF16),
            jax.ShapeDtypeStruct((d, f), BF16),
            jax.ShapeDtypeStruct((f, d), BF16),
        ],
        scratch_shapes=[pltpu.VMEM((tm, d), BF16)],
        compiler_params=pltpu.CompilerParams(
            dimension_semantics=("arbitrary",),
            vmem_limit_bytes=_vmem_limit(blocks, tm * d * 2, 4 * tm * tf * 4 + 3 * d * tf * 2)),
        name="ffn_head",
    )(x, norm_w, wg, wu, wd)


def _inproj_kernel(x_ref, nw_ref, wt_ref, wst_ref, o_ref, os_ref, h_ref):
    j = pl.program_id(1)

    @pl.when(j == 0)
    def _():
        hb = _rmsnorm(x_ref[...], nw_ref[...]).astype(BF16)
        h_ref[...] = hb
        os_ref[...] = lax.dot_general(hb, wst_ref[...], (((1,), (1,)), ((), ())), preferred_element_type=F32)

    o_ref[...] = lax.dot_general(h_ref[...], wt_ref[...], (((1,), (1,)), ((), ())), preferred_element_type=F32)


def _inproj(x, norm_w, w_in_t, w_small_t, n_main):
    m, d = x.shape
    tm, tn = INPROJ_TM, INPROJ_TN
    blocks = tm * d * 4 + d * tn * 2 + d * SMALL_COLS * 2 + tm * tn * 4 + tm * SMALL_COLS * 4 + d * 4
    return pl.pallas_call(
        _inproj_kernel,
        grid=(m // tm, n_main // tn),
        in_specs=[
            pl.BlockSpec((tm, d), lambda i, j: (i, 0)),
            pl.BlockSpec((1, d), lambda i, j: (0, 0)),
            pl.BlockSpec((tn, d), lambda i, j: (j, 0)),
            pl.BlockSpec((SMALL_COLS, d), lambda i, j: (0, 0)),
        ],
        out_specs=[
            pl.BlockSpec((tm, tn), lambda i, j: (i, j)),
            pl.BlockSpec((tm, SMALL_COLS), lambda i, j: (i, 0)),
        ],
        out_shape=[
            jax.ShapeDtypeStruct((m, n_main), F32),
            jax.ShapeDtypeStruct((m, SMALL_COLS), F32),
        ],
        scratch_shapes=[pltpu.VMEM((tm, d), BF16)],
        compiler_params=pltpu.CompilerParams(
            dimension_semantics=("parallel", "arbitrary"),
            vmem_limit_bytes=_vmem_limit(blocks, tm * d * 2, tm * d * 4 + tm * tn * 4)),
        name="inproj",
    )(x, norm_w, w_in_t, w_small_t)


def _retention_kernel(pos_ref, invf_ref, sgn_ref, q_ref, k_ref, v_ref, g_ref, o_ref,
                      state_ref, decay_ref, qdec_ref, kdec_ref, *, tb):
    heads = range(RET_HEADS)
    lanes = lambda h: slice(h * HEAD_DIM, (h + 1) * HEAD_DIM)
    log_gamma = [math.log(1.0 - 2.0 ** (-5 - h)) for h in heads]

    @pl.when((pl.program_id(0) == 0) & (pl.program_id(1) == 0))
    def _():
        ri = lax.broadcasted_iota(jnp.int32, (tb, tb), 0)
        ci = lax.broadcasted_iota(jnp.int32, (tb, tb), 1)
        causal = ri >= ci
        rel = jnp.where(causal, (ri - ci).astype(F32), 0.0)
        idx = lax.broadcasted_iota(jnp.int32, (tb, HEAD_DIM), 0).astype(F32)
        for h in heads:
            decay_ref[h] = jnp.where(causal, jnp.exp(log_gamma[h] * rel), 0.0)
            qdec_ref[h] = jnp.exp(log_gamma[h] * (idx + 1.0))
            kdec_ref[h] = jnp.exp(log_gamma[h] * (tb - 1.0 - idx))

    @pl.when(pl.program_id(1) == 0)
    def _():
        state_ref[...] = jnp.zeros_like(state_ref)

    pos = pos_ref[0].astype(F32)
    pos_rows = jnp.broadcast_to(pos, (V7X_LANES, tb)).T
    ang = pos_rows * invf_ref[...]
    cos2 = jnp.cos(ang)
    sin2 = jnp.sin(ang) * sgn_ref[...]
    scale = HEAD_DIM ** -0.5

    rot = lambda x: x * cos2 + pltpu.roll(x, HEAD_DIM // 2, 1) * sin2
    qr = [rot(q_ref[:, lanes(h)]) for h in heads]
    kr = [rot(k_ref[:, lanes(h)]) * scale for h in heads]
    v = [v_ref[:, lanes(h)] for h in heads]
    state = [state_ref[h] for h in heads]
    scores = [_mm_nt(qr[h], kr[h]) for h in heads]
    inter = [_mm(qr[h] * qdec_ref[h], state[h]) for h in heads]
    kv = [_mm_tn(kr[h] * kdec_ref[h], v[h]) for h in heads]
    intra = [_mm(scores[h] * decay_ref[h], v[h]) for h in heads]
    for h in heads:
        state_ref[h] = math.exp(log_gamma[h] * tb) * state[h] + kv[h]
        o = intra[h] + inter[h]
        mu = jnp.mean(o, axis=-1, keepdims=True)
        oc = o - mu
        var = jnp.mean(oc * oc, axis=-1, keepdims=True)
        o_ref[:, lanes(h)] = (oc * lax.rsqrt(var + EPS) * _silu(g_ref[:, lanes(h)])).astype(BF16)


def _retention(proj, pos3, invf2, sgn2, batch, seq):
    tb = RET_TB
    nt = seq // tb
    m = batch * seq
    col = lambda c: pl.BlockSpec((tb, RET_WIDTH), lambda b, t, c=c: (b * nt + t, c))
    blocks = 4 * tb * RET_WIDTH * 4 + tb * RET_WIDTH * 2 + tb * 4 + 2 * V7X_LANES * 4
    scratch = RET_HEADS * (HEAD_DIM * HEAD_DIM + tb * tb + 2 * tb * HEAD_DIM) * 4
    return pl.pallas_call(
        functools.partial(_retention_kernel, tb=tb),
        grid=(batch, nt),
        in_specs=[
            pl.BlockSpec((1, 1, tb), lambda b, t: (b * nt + t, 0, 0)),
            pl.BlockSpec((1, V7X_LANES), lambda b, t: (0, 0)),
            pl.BlockSpec((1, V7X_LANES), lambda b, t: (0, 0)),
            col(0), col(1), col(2), col(3),
        ],
        out_specs=pl.BlockSpec((tb, RET_WIDTH), lambda b, t: (b * nt + t, 0)),
        out_shape=jax.ShapeDtypeStruct((m, RET_WIDTH), BF16),
        scratch_shapes=[
            pltpu.VMEM((RET_HEADS, HEAD_DIM, HEAD_DIM), F32),
            pltpu.VMEM((RET_HEADS, tb, tb), F32),
            pltpu.VMEM((RET_HEADS, tb, HEAD_DIM), F32),
            pltpu.VMEM((RET_HEADS, tb, HEAD_DIM), F32),
        ],
        compiler_params=pltpu.CompilerParams(
            dimension_semantics=("arbitrary", "arbitrary"),
            vmem_limit_bytes=_vmem_limit(blocks, scratch, 16 * tb * tb * 4 + 8 * tb * RET_WIDTH * 4)),
        name="retention",
    )(pos3, invf2, sgn2, proj, proj, proj, proj)


def _softplus(x):
    return jnp.maximum(x, 0.0) + jnp.log1p(jnp.exp(-jnp.abs(x)))


def _unit_lower_inverses(ms, eye, blk16, blk32):
    d = [jnp.where(blk16, m, 0.0) for m in ms]
    d2 = [_mm(a, a) for a in d]
    d4 = [_mm(a, a) for a in d2]
    d8 = [_mm(a, a) for a in d4]
    p = [_mm(eye - a, eye + b) for a, b in zip(d, d2)]
    p = [_mm(a, eye + b) for a, b in zip(p, d4)]
    p = [_mm(a, eye + b) for a, b in zip(p, d8)]
    off16 = blk32 & jnp.logical_not(blk16)
    lp = [_mm(jnp.where(off16, m, 0.0), a) for m, a in zip(ms, p)]
    p = [a - _mm(a, b) for a, b in zip(p, lp)]
    lp = [_mm(jnp.where(blk32, 0.0, m), a) for m, a in zip(ms, p)]
    p = [a - _mm(a, b) for a, b in zip(p, lp)]
    return p


def _gdn_kernel(q_ref, k_ref, v_ref, z_ref, s_ref, cw_ref, alog_ref, dtb_ref, nw_ref,
                o_ref, pad_ref, conv_ref, state_ref, *, tb):
    c_len = GDN_CHUNK
    nc = tb // c_len
    halo = V7X_SUBLANES

    @pl.when(pl.program_id(1) == 0)
    def _():
        state_ref[...] = jnp.zeros_like(state_ref)
        pad_ref[:, 0:halo, :] = jnp.zeros((3, halo, GDN_WIDTH), F32)

    groups = [(0, halo)] + [(lo, min(CONV_ROWS, tb - lo)) for lo in range(halo, tb, CONV_ROWS)]
    for a, ref in enumerate((q_ref, k_ref, v_ref)):
        w = cw_ref[:, a * GDN_WIDTH:(a + 1) * GDN_WIDTH]
        taps = [w[CONV_K - 1 - back:CONV_K - back, :] for back in range(CONV_K)]
        pad_ref[a, halo:2 * halo, :] = ref[0:halo, :]
        for lo, n in groups:
            acc = ref[lo:lo + n, :] * taps[0]
            for back in range(1, CONV_K):
                prev = pad_ref[a, halo - back:2 * halo - back, :] if lo == 0 else ref[lo - back:lo - back + n, :]
                acc = acc + prev * taps[back]
            conv_ref[a, lo:lo + n, :] = _silu(acc)
        pad_ref[a, 0:halo, :] = ref[tb - halo:tb, :]

    small = s_ref[...]
    lane = lax.broadcasted_iota(jnp.int32, (tb, SMALL_COLS), 1)
    g_all = jnp.where(lane < GDN_HEADS, -jnp.exp(alog_ref[...]) * _softplus(small + dtb_ref[...]), 0.0)
    beta_all = jax.nn.sigmoid(small)

    ri = lax.broadcasted_iota(jnp.int32, (tb, tb), 0)
    ci = lax.broadcasted_iota(jnp.int32, (tb, tb), 1)
    chunk_tri = ((ri >= ci) & ((ri >> 6) == (ci >> 6))).astype(F32)
    gc = jnp.dot(chunk_tri, g_all, precision=lax.Precision.HIGHEST, preferred_element_type=F32)
    gct = gc.T

    r64 = lax.broadcasted_iota(jnp.int32, (c_len, c_len), 0)
    c64 = lax.broadcasted_iota(jnp.int32, (c_len, c_len), 1)
    causal = r64 >= c64
    strict = r64 > c64
    eye = (r64 == c64).astype(F32)
    blk16 = (r64 >> 4) == (c64 >> 4)
    blk32 = (r64 >> 5) == (c64 >> 5)
    scale = HEAD_DIM ** -0.5

    heads = range(GDN_HEADS)
    chains = [(h, c) for h in heads for c in range(nc)]
    rows = lambda c: slice(c * c_len, (c + 1) * c_len)
    lanes = lambda h: slice(h * HEAD_DIM, (h + 1) * HEAD_DIM)

    qn, kn, kb, vb, kbe, qe, g_b = [], [], [], [], [], [], []
    for h in heads:
        qh, kh, vh = conv_ref[0, :, lanes(h)], conv_ref[1, :, lanes(h)], conv_ref[2, :, lanes(h)]
        g_h = jnp.broadcast_to(gc[:, h:h + 1], (tb, HEAD_DIM))
        b_h = jnp.broadcast_to(beta_all[:, GDN_HEADS + h:GDN_HEADS + h + 1], (tb, HEAD_DIM))
        e_h = jnp.exp(g_h)
        q_h = qh * lax.rsqrt(jnp.sum(qh * qh, axis=-1, keepdims=True) + EPS) * scale
        k_h = kh * lax.rsqrt(jnp.sum(kh * kh, axis=-1, keepdims=True) + EPS)
        kb_h = k_h * b_h
        qn.append(q_h), kn.append(k_h), kb.append(kb_h), vb.append(vh * b_h)
        kbe.append(kb_h * e_h), qe.append(q_h * e_h), g_b.append(g_h)

    decay = [jnp.exp(jnp.where(causal, g_b[h][rows(c), :c_len] - gct[h:h + 1, rows(c)], -jnp.inf))
             for h, c in chains]
    kq = [_mm_nt(jnp.concatenate([kb[h][rows(c)], qn[h][rows(c)]], axis=0), kn[h][rows(c)])
          for h, c in chains]
    ms = [jnp.where(strict, a[:c_len] * dc, 0.0) for a, dc in zip(kq, decay)]
    attn = [a[c_len:] * dc for a, dc in zip(kq, decay)]
    t_inv = _unit_lower_inverses(ms, eye, blk16, blk32)
    rhs = [jnp.concatenate([vb[h][rows(c)], kbe[h][rows(c)]], axis=1) for h, c in chains]
    sol = [r + _mm(t - eye, r) for t, r in zip(t_inv, rhs)]

    state = [state_ref[h] for h in heads]
    outs = [[] for _ in heads]
    for c in range(nc):
        last = (c + 1) * c_len - 1
        g_last = [g_b[h][last:last + 1, :] for h in heads]
        k_dec = [kn[h][rows(c)] * jnp.exp(g_last[h] - g_b[h][rows(c)]) for h in heads]
        w_s = [_mm(sol[h * nc + c][:, HEAD_DIM:], state[h]) for h in heads]
        q_s = [_mm(qe[h][rows(c)], state[h]) for h in heads]
        v_new = [sol[h * nc + c][:, :HEAD_DIM] - w_s[h] for h in heads]
        a_v = [_mm(attn[h * nc + c], v_new[h]) for h in heads]
        k_v = [_mm_tn(k_dec[h], v_new[h]) for h in heads]
        state = [state[h] * jnp.exp(g_last[h]) + k_v[h] for h in heads]
        for h in heads:
            outs[h].append(q_s[h] + a_v[h])

    for h in heads:
        state_ref[h] = state[h]
        o = jnp.concatenate(outs[h], axis=0)
        o = o * lax.rsqrt(jnp.mean(o * o, axis=-1, keepdims=True) + EPS) * nw_ref[...]
        o_ref[:, lanes(h)] = (o * _silu(z_ref[:, lanes(h)])).astype(BF16)


def _gdn(proj, small, conv_w, alog_row, dtb_row, norm_w, batch, seq):
    tb = GDN_TB
    nt = seq // tb
    m = batch * seq
    first = RET_WIDTH * 4 // GDN_WIDTH
    col = lambda c: pl.BlockSpec((tb, GDN_WIDTH), lambda b, t, c=c: (b * nt + t, first + c))
    row = lambda n: pl.BlockSpec((1, n), lambda b, t: (0, 0))
    blocks = (4 * tb * GDN_WIDTH * 4 + tb * SMALL_COLS * 4 + CONV_K * 3 * GDN_WIDTH * 4
              + tb * GDN_WIDTH * 2 + 3 * V7X_LANES * 4)
    scratch = (3 * 2 * V7X_SUBLANES * GDN_WIDTH * 4 + 3 * tb * GDN_WIDTH * 4
               + GDN_HEADS * HEAD_DIM * HEAD_DIM * 4)
    return pl.pallas_call(
        functools.partial(_gdn_kernel, tb=tb),
        grid=(batch, nt),
        in_specs=[
            col(0), col(1), col(2), col(3),
            pl.BlockSpec((tb, SMALL_COLS), lambda b, t: (b * nt + t, 0)),
            pl.BlockSpec((CONV_K, 3 * GDN_WIDTH), lambda b, t: (0, 0)),
            row(SMALL_COLS), row(SMALL_COLS), row(HEAD_DIM),
        ],
        out_specs=pl.BlockSpec((tb, GDN_WIDTH), lambda b, t: (b * nt + t, 0)),
        out_shape=jax.ShapeDtypeStruct((m, GDN_WIDTH), BF16),
        scratch_shapes=[
            pltpu.VMEM((3, 2 * V7X_SUBLANES, GDN_WIDTH), F32),
            pltpu.VMEM((3, tb, GDN_WIDTH), F32),
            pltpu.VMEM((GDN_HEADS, HEAD_DIM, HEAD_DIM), F32),
        ],
        compiler_params=pltpu.CompilerParams(
            dimension_semantics=("arbitrary", "arbitrary"),
            vmem_limit_bytes=_vmem_limit(blocks, scratch, 12 * tb * GDN_WIDTH * 4)),
        name="gdn",
    )(proj, proj, proj, proj, small, conv_w, alog_row, dtb_row, norm_w)


def _outproj_kernel(x_ref, ro_ref, go_ref, wr_ref, wg_ref, o_ref):
    o_ref[...] = (x_ref[...]
                  + jnp.dot(ro_ref[...], wr_ref[...], preferred_element_type=F32)
                  + jnp.dot(go_ref[...], wg_ref[...], preferred_element_type=F32))


def _outproj(x, ro, go, w_out):
    m, d = x.shape
    tm = OUTPROJ_TM
    blocks = 2 * tm * d * 4 + tm * (RET_WIDTH + GDN_WIDTH) * 2 + (RET_WIDTH + GDN_WIDTH) * d * 2
    return pl.pallas_call(
        _outproj_kernel,
        grid=(m // tm,),
        in_specs=[
            pl.BlockSpec((tm, d), lambda i: (i, 0)),
            pl.BlockSpec((tm, RET_WIDTH), lambda i: (i, 0)),
            pl.BlockSpec((tm, GDN_WIDTH), lambda i: (i, 0)),
            pl.BlockSpec((RET_WIDTH, d), lambda i: (0, 0)),
            pl.BlockSpec((GDN_WIDTH, d), lambda i: (RET_WIDTH // GDN_WIDTH, 0)),
        ],
        out_specs=pl.BlockSpec((tm, d), lambda i: (i, 0)),
        out_shape=jax.ShapeDtypeStruct((m, d), F32),
        compiler_params=pltpu.CompilerParams(
            dimension_semantics=("parallel",),
            vmem_limit_bytes=_vmem_limit(blocks, 0, 2 * tm * d * 4)),
        name="outproj",
    )(x, ro, go, w_out, w_out)


def kernel(x, positions, norm_ffn1_w, ffn1_w_gate, ffn1_w_up, ffn1_w_down, norm_mix_w, w_in, conv_w,
           gdn_a_log, gdn_dt_bias, gdn_norm_w, w_out, norm_ffn2_w, ffn2_w_gate, ffn2_w_up, ffn2_w_down,
           norm_final_w):
    batch, seq, d = x.shape
    depth = norm_ffn1_w.shape[0]
    m = batch * seq
    n_main = 4 * RET_WIDTH + 4 * GDN_WIDTH
    row = lambda v: v.reshape(1, -1).astype(F32)

    half = HEAD_DIM // 2
    inv_freq = ROPE_THETA ** (-jnp.arange(half, dtype=F32) / half)
    invf2 = jnp.concatenate([inv_freq, inv_freq]).reshape(1, HEAD_DIM)
    sgn2 = jnp.concatenate([-jnp.ones((half,), F32), jnp.ones((half,), F32)]).reshape(1, HEAD_DIM)
    pos3 = positions.reshape(m // RET_TB, 1, RET_TB)
    pad_heads = lambda v: jnp.pad(v.astype(F32), (0, SMALL_COLS - GDN_HEADS)).reshape(1, SMALL_COLS)

    xf = x.reshape(m, d)
    for l in range(depth):
        last = l == depth - 1
        w_in_t = jnp.swapaxes(w_in[l], 0, 1)
        later = ((w_in_t, n_main), (w_out[l], w_out.shape[1]), (ffn2_w_gate[l], d), (ffn2_w_up[l], d),
                 (ffn2_w_down[l], ffn2_w_down.shape[1]))
        head, wg1, wu1, wd1 = _ffn_head(xf, row(norm_ffn1_w[l]), ffn1_w_gate[l], ffn1_w_up[l], ffn1_w_down[l])
        xf, (w_in_b, w_out_b, wg2, wu2, wd2) = _ffn(
            xf, row(norm_ffn1_w[l]), wg1, wu1, wd1, row(norm_final_w), later, final_norm=False, done=(head, 1))
        w_small = jnp.pad(w_in_t[n_main:], ((0, SMALL_COLS - 2 * GDN_HEADS), (0, 0))).astype(BF16)
        proj, small = _inproj(xf, row(norm_mix_w[l]), w_in_b, w_small, n_main)
        ro = _retention(proj, pos3, invf2, sgn2, batch, seq)
        go = _gdn(proj, small, conv_w[l].astype(F32), pad_heads(gdn_a_log[l]), pad_heads(gdn_dt_bias[l]),
                  row(gdn_norm_w[l]), batch, seq)
        xf = _outproj(xf, ro, go, w_out_b)
        xf, _ = _ffn(xf, row(norm_ffn2_w[l]), wg2, wu2, wd2, row(norm_final_w), final_norm=last)
    return xf.reshape(batch, seq, d)
```

```python
import functools
import math

import jax
import jax.numpy as jnp
from jax import lax
from jax.experimental import pallas as pl
from jax.experimental.pallas import tpu as pltpu

F32 = jnp.float32
BF16 = jnp.bfloat16

D_MODEL = 2048
HEAD_DIM = 128
RET_HEADS = 8
GDN_HEADS = 8
RET_WIDTH = RET_HEADS * HEAD_DIM
GDN_WIDTH = GDN_HEADS * HEAD_DIM
D_FF = 5632
GDN_CHUNK = 64
CONV_K = 4
ROPE_THETA = 10000.0
EPS = 1e-6

V7X_LANES = 128
V7X_SUBLANES = 8
V7X_BF16_SUBLANES = 16
V7X_VMEM_BYTES = 64 * 2**20

FFN_TM, FFN_TF = 1024, 512
FFN_HEAD_TF = 256
INPROJ_TM, INPROJ_TN = 1024, 2048
OUTPROJ_TM = 512
RET_TB = 256
GDN_TB = 256
CONV_ROWS = 32
SMALL_COLS = V7X_LANES


def _vmem_limit(pipelined_bytes, scratch_bytes, temp_bytes):
    need = 2 * pipelined_bytes + scratch_bytes + temp_bytes
    return int(min(need, V7X_VMEM_BYTES * 15 // 16))


def _mm(a, b):
    return jnp.dot(a.astype(BF16), b.astype(BF16), preferred_element_type=F32)


def _mm_nt(a, b):
    return lax.dot_general(a.astype(BF16), b.astype(BF16), (((1,), (1,)), ((), ())),
                           preferred_element_type=F32)


def _mm_tn(a, b):
    return lax.dot_general(a.astype(BF16), b.astype(BF16), (((0,), (0,)), ((), ())),
                           preferred_element_type=F32)


def _rmsnorm(x, w):
    return x * lax.rsqrt(jnp.mean(x * x, axis=-1, keepdims=True) + EPS) * w


def _silu(x):
    return x * jax.nn.sigmoid(x)


def _ffn_kernel(x_ref, nw_ref, wg_ref, wu_ref, wd_ref, fw_ref, *rest, final_norm, n_cast):
    cast_src = rest[:n_cast]
    o_ref = rest[n_cast]
    cast_dst = rest[n_cast + 1:2 * n_cast + 1]
    h_ref = rest[2 * n_cast + 1]
    j = pl.program_id(1)

    @pl.when(j == 0)
    def _():
        x = x_ref[...]
        h_ref[...] = _rmsnorm(x, nw_ref[...]).astype(BF16)
        o_ref[...] = x

    h = h_ref[...]
    g = jnp.dot(h, wg_ref[...], preferred_element_type=F32)
    u = jnp.dot(h, wu_ref[...], preferred_element_type=F32)
    act = (_silu(g) * u) * 0.5
    o_ref[...] += jnp.dot(act.astype(BF16), wd_ref[...], preferred_element_type=F32)
    for src, dst in zip(cast_src, cast_dst):
        dst[...] = src[...].astype(BF16)

    if final_norm:
        @pl.when(j == pl.num_programs(1) - 1)
        def _():
            o_ref[...] = _rmsnorm(o_ref[...], fw_ref[...])


def _cast_plan(w, rows, n_steps):
    slab = V7X_BF16_SUBLANES * pl.cdiv(rows, V7X_BF16_SUBLANES * n_steps)
    while rows % slab:
        slab += V7X_BF16_SUBLANES
    return w, slab, rows // slab


def _ffn(x, norm_w, wg, wu, wd, final_w, cast_weights=(), *, final_norm, done=None):
    m, d = x.shape
    f = wg.shape[1]
    tm, tf = FFN_TM, FFN_TF
    skip = 0 if done is None else done[1]
    ni, nj = m // tm - skip, f // tf
    plans = [_cast_plan(w, rows, ni * nj) for w, rows in cast_weights]
    cast_specs = [pl.BlockSpec((r, v.shape[1]), lambda i, j, used=used: (jnp.minimum(i * nj + j, used - 1), 0))
                  for v, r, used in plans]
    cast_bytes = sum(r * v.shape[1] * (4 + 2) for v, r, _ in plans)
    blocks = 2 * tm * d * 4 + 3 * d * tf * 2 + 2 * d * 4 + cast_bytes
    operands = [x, norm_w, wg, wu, wd, final_w] + [v for v, _, _ in plans]
    in_specs = [
        pl.BlockSpec((tm, d), lambda i, j: (i + skip, 0)),
        pl.BlockSpec((1, d), lambda i, j: (0, 0)),
        pl.BlockSpec((d, tf), lambda i, j: (0, j)),
        pl.BlockSpec((d, tf), lambda i, j: (0, j)),
        pl.BlockSpec((tf, d), lambda i, j: (j, 0)),
        pl.BlockSpec((1, d), lambda i, j: (0, 0)),
    ] + cast_specs
    body = functools.partial(_ffn_kernel, final_norm=final_norm, n_cast=len(plans))
    aliases = {}
    if done is not None:
        operands.append(done[0])
        in_specs.append(pl.BlockSpec(memory_space=pl.ANY))
        aliases = {len(operands) - 1: 0}
        body = functools.partial(_drop_operand, body, len(operands) - 1)
    outs = pl.pallas_call(
        body,
        grid=(ni, nj),
        in_specs=in_specs,
        out_specs=[pl.BlockSpec((tm, d), lambda i, j: (i + skip, 0))] + cast_specs,
        out_shape=[jax.ShapeDtypeStruct((m, d), F32)]
        + [jax.ShapeDtypeStruct((used * r, v.shape[1]), BF16) for v, r, used in plans],
        scratch_shapes=[pltpu.VMEM((tm, d), BF16)],
        input_output_aliases=aliases,
        compiler_params=pltpu.CompilerParams(
            dimension_semantics=("arbitrary", "arbitrary"),
            vmem_limit_bytes=_vmem_limit(blocks, tm * d * 2, 4 * tm * tf * 4)),
        name="ffn_final" if final_norm else "ffn",
    )(*operands)
    return outs[0], list(outs[1:])


def _drop_operand(body, index, *refs, **kwargs):
    return body(*refs[:index], *refs[index + 1:], **kwargs)


def _ffn_head_kernel(x_ref, nw_ref, wg_ref, wu_ref, wd_ref, o_ref, wgb_ref, wub_ref, wdb_ref, h_ref):
    j = pl.program_id(0)

    @pl.when(j == 0)
    def _():
        x = x_ref[...]
        h_ref[...] = _rmsnorm(x, nw_ref[...]).astype(BF16)
        o_ref[...] = x

    wg = wg_ref[...].astype(BF16)
    wu = wu_ref[...].astype(BF16)
    wd = wd_ref[...].astype(BF16)
    wgb_ref[...] = wg
    wub_ref[...] = wu
    wdb_ref[...] = wd
    h = h_ref[...]
    g = jnp.dot(h, wg, preferred_element_type=F32)
    u = jnp.dot(h, wu, preferred_element_type=F32)
    act = (_silu(g) * u) * 0.5
    o_ref[...] += jnp.dot(act.astype(BF16), wd, preferred_element_type=F32)


def _ffn_head(x, norm_w, wg, wu, wd):
    m, d = x.shape
    f = wg.shape[1]
    tm, tf = FFN_TM, FFN_HEAD_TF
    blocks = 2 * tm * d * 4 + 3 * d * tf * (4 + 2) + d * 4
    return pl.pallas_call(
        _ffn_head_kernel,
        grid=(f // tf,),
        in_specs=[
            pl.BlockSpec((tm, d), lambda j: (0, 0)),
            pl.BlockSpec((1, d), lambda j: (0, 0)),
            pl.BlockSpec((d, tf), lambda j: (0, j)),
            pl.BlockSpec((d, tf), lambda j: (0, j)),
            pl.BlockSpec((tf, d), lambda j: (j, 0)),
        ],
        out_specs=[
            pl.BlockSpec((tm, d), lambda j: (0, 0)),
            pl.BlockSpec((d, tf), lambda j: (0, j)),
            pl.BlockSpec((d, tf), lambda j: (0, j)),
            pl.BlockSpec((tf, d), lambda j: (j, 0)),
        ],
        out_shape=[
            jax.ShapeDtypeStruct((m, d), F32),
            jax.ShapeDtypeStruct((d, f), BF16),
            jax.ShapeDtypeStruct((d, f), BF16),
            jax.ShapeDtypeStruct((f, d), BF16),
        ],
        scratch_shapes=[pltpu.VMEM((tm, d), BF16)],
        compiler_params=pltpu.CompilerParams(
            dimension_semantics=("arbitrary",),
            vmem_limit_bytes=_vmem_limit(blocks, tm * d * 2, 4 * tm * tf * 4 + 3 * d * tf * 2)),
        name="ffn_head",
    )(x, norm_w, wg, wu, wd)


def _inproj_kernel(x_ref, nw_ref, wt_ref, wst_ref, o_ref, os_ref, h_ref):
    j = pl.program_id(1)

    @pl.when(j == 0)
    def _():
        hb = _rmsnorm(x_ref[...], nw_ref[...]).astype(BF16)
        h_ref[...] = hb
        os_ref[...] = lax.dot_general(hb, wst_ref[...], (((1,), (1,)), ((), ())), preferred_element_type=F32)

    o_ref[...] = lax.dot_general(h_ref[...], wt_ref[...], (((1,), (1,)), ((), ())), preferred_element_type=F32)


def _inproj(x, norm_w, w_in_t, w_small_t, n_main):
    m, d = x.shape
    tm, tn = INPROJ_TM, INPROJ_TN
    blocks = tm * d * 4 + d * tn * 2 + d * SMALL_COLS * 2 + tm * tn * 4 + tm * SMALL_COLS * 4 + d * 4
    return pl.pallas_call(
        _inproj_kernel,
        grid=(m // tm, n_main // tn),
        in_specs=[
            pl.BlockSpec((tm, d), lambda i, j: (i, 0)),
            pl.BlockSpec((1, d), lambda i, j: (0, 0)),
            pl.BlockSpec((tn, d), lambda i, j: (j, 0)),
            pl.BlockSpec((SMALL_COLS, d), lambda i, j: (0, 0)),
        ],
        out_specs=[
            pl.BlockSpec((tm, tn), lambda i, j: (i, j)),
            pl.BlockSpec((tm, SMALL_COLS), lambda i, j: (i, 0)),
        ],
        out_shape=[
            jax.ShapeDtypeStruct((m, n_main), F32),
            jax.ShapeDtypeStruct((m, SMALL_COLS), F32),
        ],
        scratch_shapes=[pltpu.VMEM((tm, d), BF16)],
        compiler_params=pltpu.CompilerParams(
            dimension_semantics=("parallel", "arbitrary"),
            vmem_limit_bytes=_vmem_limit(blocks, tm * d * 2, tm * d * 4 + tm * tn * 4)),
        name="inproj",
    )(x, norm_w, w_in_t, w_small_t)


def _retention_log_gamma():
    return [math.log(1.0 - 2.0 ** (-5 - h)) for h in range(RET_HEADS)]


def _retention_setup(state_ref, decay_ref, qdec_ref, kdec_ref, *, tb):
    log_gamma = _retention_log_gamma()

    @pl.when((pl.program_id(0) == 0) & (pl.program_id(1) == 0))
    def _():
        ri = lax.broadcasted_iota(jnp.int32, (tb, tb), 0)
        ci = lax.broadcasted_iota(jnp.int32, (tb, tb), 1)
        causal = ri >= ci
        rel = jnp.where(causal, (ri - ci).astype(F32), 0.0)
        idx = lax.broadcasted_iota(jnp.int32, (tb, HEAD_DIM), 0).astype(F32)
        for h in range(RET_HEADS):
            decay_ref[h] = jnp.where(causal, jnp.exp(log_gamma[h] * rel), 0.0)
            qdec_ref[h] = jnp.exp(log_gamma[h] * (idx + 1.0))
            kdec_ref[h] = jnp.exp(log_gamma[h] * (tb - 1.0 - idx))

    @pl.when(pl.program_id(1) == 0)
    def _():
        state_ref[...] = jnp.zeros_like(state_ref)


def _retention_step(pos_ref, invf_ref, sgn_ref, q_ref, k_ref, v_ref, g_ref, o_ref,
                    state_ref, decay_ref, qdec_ref, kdec_ref, *, tb):
    heads = range(RET_HEADS)
    lanes = lambda h: slice(h * HEAD_DIM, (h + 1) * HEAD_DIM)
    log_gamma = _retention_log_gamma()

    pos = pos_ref[0].astype(F32)
    pos_rows = jnp.broadcast_to(pos, (V7X_LANES, tb)).T
    ang = pos_rows * invf_ref[...]
    cos2 = jnp.cos(ang)
    sin2 = jnp.sin(ang) * sgn_ref[...]
    scale = HEAD_DIM ** -0.5

    rot = lambda x: x * cos2 + pltpu.roll(x, HEAD_DIM // 2, 1) * sin2
    qr = [rot(q_ref[:, lanes(h)]) for h in heads]
    kr = [rot(k_ref[:, lanes(h)]) * scale for h in heads]
    v = [v_ref[:, lanes(h)] for h in heads]
    state = [state_ref[h] for h in heads]
    scores = [_mm_nt(qr[h], kr[h]) for h in heads]
    inter = [_mm(qr[h] * qdec_ref[h], state[h]) for h in heads]
    kv = [_mm_tn(kr[h] * kdec_ref[h], v[h]) for h in heads]
    intra = [_mm(scores[h] * decay_ref[h], v[h]) for h in heads]
    for h in heads:
        state_ref[h] = math.exp(log_gamma[h] * tb) * state[h] + kv[h]
        o = intra[h] + inter[h]
        mu = jnp.mean(o, axis=-1, keepdims=True)
        oc = o - mu
        var = jnp.mean(oc * oc, axis=-1, keepdims=True)
        o_ref[:, lanes(h)] = (oc * lax.rsqrt(var + EPS) * _silu(g_ref[:, lanes(h)])).astype(BF16)


def _softplus(x):
    return jnp.maximum(x, 0.0) + jnp.log1p(jnp.exp(-jnp.abs(x)))


def _unit_lower_inverses(ms, eye, blk16, blk32):
    d = [jnp.where(blk16, m, 0.0) for m in ms]
    d2 = [_mm(a, a) for a in d]
    d4 = [_mm(a, a) for a in d2]
    d8 = [_mm(a, a) for a in d4]
    p = [_mm(eye - a, eye + b) for a, b in zip(d, d2)]
    p = [_mm(a, eye + b) for a, b in zip(p, d4)]
    p = [_mm(a, eye + b) for a, b in zip(p, d8)]
    off16 = blk32 & jnp.logical_not(blk16)
    lp = [_mm(jnp.where(off16, m, 0.0), a) for m, a in zip(ms, p)]
    p = [a - _mm(a, b) for a, b in zip(p, lp)]
    lp = [_mm(jnp.where(blk32, 0.0, m), a) for m, a in zip(ms, p)]
    p = [a - _mm(a, b) for a, b in zip(p, lp)]
    return p


def _gdn_setup(pad_ref, state_ref):
    @pl.when(pl.program_id(1) == 0)
    def _():
        state_ref[...] = jnp.zeros_like(state_ref)
        pad_ref[:, 0:V7X_SUBLANES, :] = jnp.zeros((3, V7X_SUBLANES, GDN_WIDTH), F32)


def _gdn_step(q_ref, k_ref, v_ref, z_ref, s_ref, cw_ref, alog_ref, dtb_ref, nw_ref,
              o_ref, pad_ref, conv_ref, state_ref, *, tb):
    c_len = GDN_CHUNK
    nc = tb // c_len
    halo = V7X_SUBLANES

    groups = [(0, halo)] + [(lo, min(CONV_ROWS, tb - lo)) for lo in range(halo, tb, CONV_ROWS)]
    for a, ref in enumerate((q_ref, k_ref, v_ref)):
        w = cw_ref[:, a * GDN_WIDTH:(a + 1) * GDN_WIDTH]
        taps = [w[CONV_K - 1 - back:CONV_K - back, :] for back in range(CONV_K)]
        pad_ref[a, halo:2 * halo, :] = ref[0:halo, :]
        for lo, n in groups:
            acc = ref[lo:lo + n, :] * taps[0]
            for back in range(1, CONV_K):
                prev = pad_ref[a, halo - back:2 * halo - back, :] if lo == 0 else ref[lo - back:lo - back + n, :]
                acc = acc + prev * taps[back]
            conv_ref[a, lo:lo + n, :] = _silu(acc)
        pad_ref[a, 0:halo, :] = ref[tb - halo:tb, :]

    small = s_ref[...]
    lane = lax.broadcasted_iota(jnp.int32, (tb, SMALL_COLS), 1)
    g_all = jnp.where(lane < GDN_HEADS, -jnp.exp(alog_ref[...]) * _softplus(small + dtb_ref[...]), 0.0)
    beta_all = jax.nn.sigmoid(small)

    ri = lax.broadcasted_iota(jnp.int32, (tb, tb), 0)
    ci = lax.broadcasted_iota(jnp.int32, (tb, tb), 1)
    chunk_tri = ((ri >= ci) & ((ri >> 6) == (ci >> 6))).astype(F32)
    gc = jnp.dot(chunk_tri, g_all, precision=lax.Precision.HIGHEST, preferred_element_type=F32)
    gct = gc.T

    r64 = lax.broadcasted_iota(jnp.int32, (c_len, c_len), 0)
    c64 = lax.broadcasted_iota(jnp.int32, (c_len, c_len), 1)
    causal = r64 >= c64
    strict = r64 > c64
    eye = (r64 == c64).astype(F32)
    blk16 = (r64 >> 4) == (c64 >> 4)
    blk32 = (r64 >> 5) == (c64 >> 5)
    scale = HEAD_DIM ** -0.5

    heads = range(GDN_HEADS)
    chains = [(h, c) for h in heads for c in range(nc)]
    rows = lambda c: slice(c * c_len, (c + 1) * c_len)
    lanes = lambda h: slice(h * HEAD_DIM, (h + 1) * HEAD_DIM)

    qn, kn, kb, vb, kbe, qe, g_b = [], [], [], [], [], [], []
    for h in heads:
        qh, kh, vh = conv_ref[0, :, lanes(h)], conv_ref[1, :, lanes(h)], conv_ref[2, :, lanes(h)]
        g_h = jnp.broadcast_to(gc[:, h:h + 1], (tb, HEAD_DIM))
        b_h = jnp.broadcast_to(beta_all[:, GDN_HEADS + h:GDN_HEADS + h + 1], (tb, HEAD_DIM))
        e_h = jnp.exp(g_h)
        q_h = qh * lax.rsqrt(jnp.sum(qh * qh, axis=-1, keepdims=True) + EPS) * scale
        k_h = kh * lax.rsqrt(jnp.sum(kh * kh, axis=-1, keepdims=True) + EPS)
        kb_h = k_h * b_h
        qn.append(q_h), kn.append(k_h), kb.append(kb_h), vb.append(vh * b_h)
        kbe.append(kb_h * e_h), qe.append(q_h * e_h), g_b.append(g_h)

    decay = [jnp.exp(jnp.where(causal, g_b[h][rows(c), :c_len] - gct[h:h + 1, rows(c)], -jnp.inf))
             for h, c in chains]
    kq = [_mm_nt(jnp.concatenate([kb[h][rows(c)], qn[h][rows(c)]], axis=0), kn[h][rows(c)])
          for h, c in chains]
    ms = [jnp.where(strict, a[:c_len] * dc, 0.0) for a, dc in zip(kq, decay)]
    attn = [a[c_len:] * dc for a, dc in zip(kq, decay)]
    t_inv = _unit_lower_inverses(ms, eye, blk16, blk32)
    rhs = [jnp.concatenate([vb[h][rows(c)], kbe[h][rows(c)]], axis=1) for h, c in chains]
    sol = [r + _mm(t - eye, r) for t, r in zip(t_inv, rhs)]

    state = [state_ref[h] for h in heads]
    outs = [[] for _ in heads]
    for c in range(nc):
        last = (c + 1) * c_len - 1
        g_last = [g_b[h][last:last + 1, :] for h in heads]
        k_dec = [kn[h][rows(c)] * jnp.exp(g_last[h] - g_b[h][rows(c)]) for h in heads]
        w_s = [_mm(sol[h * nc + c][:, HEAD_DIM:], state[h]) for h in heads]
        q_s = [_mm(qe[h][rows(c)], state[h]) for h in heads]
        v_new = [sol[h * nc + c][:, :HEAD_DIM] - w_s[h] for h in heads]
        a_v = [_mm(attn[h * nc + c], v_new[h]) for h in heads]
        k_v = [_mm_tn(k_dec[h], v_new[h]) for h in heads]
        state = [state[h] * jnp.exp(g_last[h]) + k_v[h] for h in heads]
        for h in heads:
            outs[h].append(q_s[h] + a_v[h])

    for h in heads:
        state_ref[h] = state[h]
        o = jnp.concatenate(outs[h], axis=0)
        o = o * lax.rsqrt(jnp.mean(o * o, axis=-1, keepdims=True) + EPS) * nw_ref[...]
        o_ref[:, lanes(h)] = (o * _silu(z_ref[:, lanes(h)])).astype(BF16)


def _mixer_kernel(pos_ref, invf_ref, sgn_ref, rq_ref, rk_ref, rv_ref, rg_ref,
                  gq_ref, gk_ref, gv_ref, gz_ref, s_ref, cw_ref, alog_ref, dtb_ref, nw_ref,
                  ro_ref, go_ref, r_state, r_decay, r_qdec, r_kdec, g_pad, g_conv, g_state, *, tb):
    _gdn_setup(g_pad, g_state)
    _retention_setup(r_state, r_decay, r_qdec, r_kdec, tb=tb)
    _gdn_step(gq_ref, gk_ref, gv_ref, gz_ref, s_ref, cw_ref, alog_ref, dtb_ref, nw_ref,
              go_ref, g_pad, g_conv, g_state, tb=tb)
    _retention_step(pos_ref, invf_ref, sgn_ref, rq_ref, rk_ref, rv_ref, rg_ref, ro_ref,
                    r_state, r_decay, r_qdec, r_kdec, tb=tb)


def _mixer(proj, small, pos3, invf2, sgn2, conv_w, alog_row, dtb_row, norm_w, batch, seq):
    assert RET_TB == GDN_TB
    tb = GDN_TB
    nt = seq // tb
    m = batch * seq
    col = lambda c: pl.BlockSpec((tb, RET_WIDTH), lambda b, t, c=c: (b * nt + t, c))
    row = lambda n: pl.BlockSpec((1, n), lambda b, t: (0, 0))
    out = pl.BlockSpec((tb, GDN_WIDTH), lambda b, t: (b * nt + t, 0))
    blocks = (8 * tb * GDN_WIDTH * 4 + tb * SMALL_COLS * 4 + CONV_K * 3 * GDN_WIDTH * 4
              + 2 * tb * GDN_WIDTH * 2 + tb * 4 + 5 * V7X_LANES * 4)
    scratch = (RET_HEADS * (HEAD_DIM * HEAD_DIM + tb * tb + 2 * tb * HEAD_DIM) * 4
               + 3 * 2 * V7X_SUBLANES * GDN_WIDTH * 4 + 3 * tb * GDN_WIDTH * 4
               + GDN_HEADS * HEAD_DIM * HEAD_DIM * 4)
    return pl.pallas_call(
        functools.partial(_mixer_kernel, tb=tb),
        grid=(batch, nt),
        in_specs=[
            pl.BlockSpec((1, 1, tb), lambda b, t: (b * nt + t, 0, 0)), row(V7X_LANES), row(V7X_LANES),
            col(0), col(1), col(2), col(3), col(4), col(5), col(6), col(7),
            pl.BlockSpec((tb, SMALL_COLS), lambda b, t: (b * nt + t, 0)),
            pl.BlockSpec((CONV_K, 3 * GDN_WIDTH), lambda b, t: (0, 0)),
            row(SMALL_COLS), row(SMALL_COLS), row(HEAD_DIM),
        ],
        out_specs=[out, out],
        out_shape=[jax.ShapeDtypeStruct((m, RET_WIDTH), BF16), jax.ShapeDtypeStruct((m, GDN_WIDTH), BF16)],
        scratch_shapes=[
            pltpu.VMEM((RET_HEADS, HEAD_DIM, HEAD_DIM), F32),
            pltpu.VMEM((RET_HEADS, tb, tb), F32),
            pltpu.VMEM((RET_HEADS, tb, HEAD_DIM), F32),
            pltpu.VMEM((RET_HEADS, tb, HEAD_DIM), F32),
            pltpu.VMEM((3, 2 * V7X_SUBLANES, GDN_WIDTH), F32),
            pltpu.VMEM((3, tb, GDN_WIDTH), F32),
            pltpu.VMEM((GDN_HEADS, HEAD_DIM, HEAD_DIM), F32),
        ],
        compiler_params=pltpu.CompilerParams(
            dimension_semantics=("arbitrary", "arbitrary"),
            vmem_limit_bytes=_vmem_limit(blocks, scratch, 16 * tb * tb * 4 + 20 * tb * GDN_WIDTH * 4)),
        name="mixer",
    )(pos3, invf2, sgn2, *([proj] * 8), small, conv_w, alog_row, dtb_row, norm_w)


def _outproj_kernel(x_ref, ro_ref, go_ref, wr_ref, wg_ref, o_ref):
    o_ref[...] = (x_ref[...]
                  + jnp.dot(ro_ref[...], wr_ref[...], preferred_element_type=F32)
                  + jnp.dot(go_ref[...], wg_ref[...], preferred_element_type=F32))


def _outproj(x, ro, go, w_out):
    m, d = x.shape
    tm = OUTPROJ_TM
    blocks = 2 * tm * d * 4 + tm * (RET_WIDTH + GDN_WIDTH) * 2 + (RET_WIDTH + GDN_WIDTH) * d * 2
    return pl.pallas_call(
        _outproj_kernel,
        grid=(m // tm,),
        in_specs=[
            pl.BlockSpec((tm, d), lambda i: (i, 0)),
            pl.BlockSpec((tm, RET_WIDTH), lambda i: (i, 0)),
            pl.BlockSpec((tm, GDN_WIDTH), lambda i: (i, 0)),
            pl.BlockSpec((RET_WIDTH, d), lambda i: (0, 0)),
            pl.BlockSpec((GDN_WIDTH, d), lambda i: (RET_WIDTH // GDN_WIDTH, 0)),
        ],
        out_specs=pl.BlockSpec((tm, d), lambda i: (i, 0)),
        out_shape=jax.ShapeDtypeStruct((m, d), F32),
        compiler_params=pltpu.CompilerParams(
            dimension_semantics=("parallel",),
            vmem_limit_bytes=_vmem_limit(blocks, 0, 2 * tm * d * 4)),
        name="outproj",
    )(x, ro, go, w_out, w_out)


def kernel(x, positions, norm_ffn1_w, ffn1_w_gate, ffn1_w_up, ffn1_w_down, norm_mix_w, w_in, conv_w,
           gdn_a_log, gdn_dt_bias, gdn_norm_w, w_out, norm_ffn2_w, ffn2_w_gate, ffn2_w_up, ffn2_w_down,
           norm_final_w):
    batch, seq, d = x.shape
    depth = norm_ffn1_w.shape[0]
    m = batch * seq
    n_main = 4 * RET_WIDTH + 4 * GDN_WIDTH
    row = lambda v: v.reshape(1, -1).astype(F32)

    half = HEAD_DIM // 2
    inv_freq = ROPE_THETA ** (-jnp.arange(half, dtype=F32) / half)
    invf2 = jnp.concatenate([inv_freq, inv_freq]).reshape(1, HEAD_DIM)
    sgn2 = jnp.concatenate([-jnp.ones((half,), F32), jnp.ones((half,), F32)]).reshape(1, HEAD_DIM)
    pos3 = positions.reshape(m // RET_TB, 1, RET_TB)
    pad_heads = lambda v: jnp.pad(v.astype(F32), (0, SMALL_COLS - GDN_HEADS)).reshape(1, SMALL_COLS)

    xf = x.reshape(m, d)
    for l in range(depth):
        last = l == depth - 1
        w_in_t = jnp.swapaxes(w_in[l], 0, 1)
        later = ((w_in_t, n_main), (w_out[l], w_out.shape[1]), (ffn2_w_gate[l], d), (ffn2_w_up[l], d),
                 (ffn2_w_down[l], ffn2_w_down.shape[1]))
        head, wg1, wu1, wd1 = _ffn_head(xf, row(norm_ffn1_w[l]), ffn1_w_gate[l], ffn1_w_up[l], ffn1_w_down[l])
        xf, (w_in_b, w_out_b, wg2, wu2, wd2) = _ffn(
            xf, row(norm_ffn1_w[l]), wg1, wu1, wd1, row(norm_final_w), later, final_norm=False, done=(head, 1))
        w_small = jnp.pad(w_in_t[n_main:], ((0, SMALL_COLS - 2 * GDN_HEADS), (0, 0))).astype(BF16)
        proj, small = _inproj(xf, row(norm_mix_w[l]), w_in_b, w_small, n_main)
        ro, go = _mixer(proj, small, pos3, invf2, sgn2, conv_w[l].astype(F32), pad_heads(gdn_a_log[l]),
                        pad_heads(gdn_dt_bias[l]), row(gdn_norm_w[l]), batch, seq)
        xf = _outproj(xf, ro, go, w_out_b)
        xf, _ = _ffn(xf, row(norm_ffn2_w[l]), wg2, wu2, wd2, row(norm_final_w), final_norm=last)
    return xf.reshape(batch, seq, d)
```

```python
import functools
import math

import jax
import jax.numpy as jnp
from jax import lax
from jax.experimental import pallas as pl
from jax.experimental.pallas import tpu as pltpu

F32 = jnp.float32
BF16 = jnp.bfloat16

D_MODEL = 2048
HEAD_DIM = 128
RET_HEADS = 8
GDN_HEADS = 8
RET_WIDTH = RET_HEADS * HEAD_DIM
GDN_WIDTH = GDN_HEADS * HEAD_DIM
D_FF = 5632
GDN_CHUNK = 64
CONV_K = 4
ROPE_THETA = 10000.0
EPS = 1e-6

V7X_LANES = 128
V7X_SUBLANES = 8
V7X_BF16_SUBLANES = 16
V7X_VMEM_BYTES = 64 * 2**20

FFN_TM, FFN_TF = 1024, 512
FFN_HEAD_TF = 256
INPROJ_TM, INPROJ_TN = 1024, 2048
RET_TB = 256
GDN_TB = 256
CONV_ROWS = 32
SMALL_COLS = V7X_LANES


def _vmem_limit(pipelined_bytes, scratch_bytes, temp_bytes):
    need = 2 * pipelined_bytes + scratch_bytes + temp_bytes
    return int(min(need, V7X_VMEM_BYTES * 15 // 16))


def _mm(a, b):
    return jnp.dot(a.astype(BF16), b.astype(BF16), preferred_element_type=F32)


def _mm_nt(a, b):
    return lax.dot_general(a.astype(BF16), b.astype(BF16), (((1,), (1,)), ((), ())),
                           preferred_element_type=F32)


def _mm_tn(a, b):
    return lax.dot_general(a.astype(BF16), b.astype(BF16), (((0,), (0,)), ((), ())),
                           preferred_element_type=F32)


def _rmsnorm(x, w):
    return x * lax.rsqrt(jnp.mean(x * x, axis=-1, keepdims=True) + EPS) * w


def _silu(x):
    return x * jax.nn.sigmoid(x)


def _ffn_kernel(x_ref, nw_ref, wg_ref, wu_ref, wd_ref, fw_ref, *rest, final_norm, n_cast):
    cast_src = rest[:n_cast]
    o_ref = rest[n_cast]
    cast_dst = rest[n_cast + 1:2 * n_cast + 1]
    h_ref = rest[2 * n_cast + 1]
    j = pl.program_id(1)

    @pl.when(j == 0)
    def _():
        x = x_ref[...]
        h_ref[...] = _rmsnorm(x, nw_ref[...]).astype(BF16)
        o_ref[...] = x

    h = h_ref[...]
    g = jnp.dot(h, wg_ref[...], preferred_element_type=F32)
    u = jnp.dot(h, wu_ref[...], preferred_element_type=F32)
    act = (_silu(g) * u) * 0.5
    o_ref[...] += jnp.dot(act.astype(BF16), wd_ref[...], preferred_element_type=F32)
    for src, dst in zip(cast_src, cast_dst):
        dst[...] = src[...].astype(BF16)

    if final_norm:
        @pl.when(j == pl.num_programs(1) - 1)
        def _():
            o_ref[...] = _rmsnorm(o_ref[...], fw_ref[...])


def _cast_plan(w, rows, n_steps):
    slab = V7X_BF16_SUBLANES * pl.cdiv(rows, V7X_BF16_SUBLANES * n_steps)
    while rows % slab:
        slab += V7X_BF16_SUBLANES
    return w, slab, rows // slab


def _ffn(x, norm_w, wg, wu, wd, final_w, cast_weights=(), *, final_norm, done=None):
    m, d = x.shape
    f = wg.shape[1]
    tm, tf = FFN_TM, FFN_TF
    skip = 0 if done is None else done[1]
    ni, nj = m // tm - skip, f // tf
    plans = [_cast_plan(w, rows, ni * nj) for w, rows in cast_weights]
    cast_specs = [pl.BlockSpec((r, v.shape[1]), lambda i, j, used=used: (jnp.minimum(i * nj + j, used - 1), 0))
                  for v, r, used in plans]
    cast_bytes = sum(r * v.shape[1] * (4 + 2) for v, r, _ in plans)
    blocks = 2 * tm * d * 4 + 3 * d * tf * 2 + 2 * d * 4 + cast_bytes
    operands = [x, norm_w, wg, wu, wd, final_w] + [v for v, _, _ in plans]
    in_specs = [
        pl.BlockSpec((tm, d), lambda i, j: (i + skip, 0)),
        pl.BlockSpec((1, d), lambda i, j: (0, 0)),
        pl.BlockSpec((d, tf), lambda i, j: (0, j)),
        pl.BlockSpec((d, tf), lambda i, j: (0, j)),
        pl.BlockSpec((tf, d), lambda i, j: (j, 0)),
        pl.BlockSpec((1, d), lambda i, j: (0, 0)),
    ] + cast_specs
    body = functools.partial(_ffn_kernel, final_norm=final_norm, n_cast=len(plans))
    aliases = {}
    if done is not None:
        operands.append(done[0])
        in_specs.append(pl.BlockSpec(memory_space=pl.ANY))
        aliases = {len(operands) - 1: 0}
        body = functools.partial(_drop_operand, body, len(operands) - 1)
    outs = pl.pallas_call(
        body,
        grid=(ni, nj),
        in_specs=in_specs,
        out_specs=[pl.BlockSpec((tm, d), lambda i, j: (i + skip, 0))] + cast_specs,
        out_shape=[jax.ShapeDtypeStruct((m, d), F32)]
        + [jax.ShapeDtypeStruct((used * r, v.shape[1]), BF16) for v, r, used in plans],
        scratch_shapes=[pltpu.VMEM((tm, d), BF16)],
        input_output_aliases=aliases,
        compiler_params=pltpu.CompilerParams(
            dimension_semantics=("arbitrary", "arbitrary"),
            vmem_limit_bytes=_vmem_limit(blocks, tm * d * 2, 4 * tm * tf * 4)),
        name="ffn_final" if final_norm else "ffn",
    )(*operands)
    return outs[0], list(outs[1:])


def _drop_operand(body, index, *refs, **kwargs):
    return body(*refs[:index], *refs[index + 1:], **kwargs)


def _ffn_head_kernel(x_ref, nw_ref, wg_ref, wu_ref, wd_ref, o_ref, wgb_ref, wub_ref, wdb_ref, h_ref):
    j = pl.program_id(0)

    @pl.when(j == 0)
    def _():
        x = x_ref[...]
        h_ref[...] = _rmsnorm(x, nw_ref[...]).astype(BF16)
        o_ref[...] = x

    wg = wg_ref[...].astype(BF16)
    wu = wu_ref[...].astype(BF16)
    wd = wd_ref[...].astype(BF16)
    wgb_ref[...] = wg
    wub_ref[...] = wu
    wdb_ref[...] = wd
    h = h_ref[...]
    g = jnp.dot(h, wg, preferred_element_type=F32)
    u = jnp.dot(h, wu, preferred_element_type=F32)
    act = (_silu(g) * u) * 0.5
    o_ref[...] += jnp.dot(act.astype(BF16), wd, preferred_element_type=F32)


def _ffn_head(x, norm_w, wg, wu, wd):
    m, d = x.shape
    f = wg.shape[1]
    tm, tf = FFN_TM, FFN_HEAD_TF
    blocks = 2 * tm * d * 4 + 3 * d * tf * (4 + 2) + d * 4
    return pl.pallas_call(
        _ffn_head_kernel,
        grid=(f // tf,),
        in_specs=[
            pl.BlockSpec((tm, d), lambda j: (0, 0)),
            pl.BlockSpec((1, d), lambda j: (0, 0)),
            pl.BlockSpec((d, tf), lambda j: (0, j)),
            pl.BlockSpec((d, tf), lambda j: (0, j)),
            pl.BlockSpec((tf, d), lambda j: (j, 0)),
        ],
        out_specs=[
            pl.BlockSpec((tm, d), lambda j: (0, 0)),
            pl.BlockSpec((d, tf), lambda j: (0, j)),
            pl.BlockSpec((d, tf), lambda j: (0, j)),
            pl.BlockSpec((tf, d), lambda j: (j, 0)),
        ],
        out_shape=[
            jax.ShapeDtypeStruct((m, d), F32),
            jax.ShapeDtypeStruct((d, f), BF16),
            jax.ShapeDtypeStruct((d, f), BF16),
            jax.ShapeDtypeStruct((f, d), BF16),
        ],
        scratch_shapes=[pltpu.VMEM((tm, d), BF16)],
        compiler_params=pltpu.CompilerParams(
            dimension_semantics=("arbitrary",),
            vmem_limit_bytes=_vmem_limit(blocks, tm * d * 2, 4 * tm * tf * 4 + 3 * d * tf * 2)),
        name="ffn_head",
    )(x, norm_w, wg, wu, wd)


def _inproj_kernel(x_ref, nw_ref, wt_ref, wst_ref, o_ref, os_ref, h_ref):
    j = pl.program_id(1)

    @pl.when(j == 0)
    def _():
        hb = _rmsnorm(x_ref[...], nw_ref[...]).astype(BF16)
        h_ref[...] = hb
        os_ref[...] = lax.dot_general(hb, wst_ref[...], (((1,), (1,)), ((), ())), preferred_element_type=F32)

    o_ref[...] = lax.dot_general(h_ref[...], wt_ref[...], (((1,), (1,)), ((), ())), preferred_element_type=F32)


def _inproj(x, norm_w, w_in_t, w_small_t, n_main):
    m, d = x.shape
    tm, tn = INPROJ_TM, INPROJ_TN
    blocks = tm * d * 4 + d * tn * 2 + d * SMALL_COLS * 2 + tm * tn * 4 + tm * SMALL_COLS * 4 + d * 4
    return pl.pallas_call(
        _inproj_kernel,
        grid=(m // tm, n_main // tn),
        in_specs=[
            pl.BlockSpec((tm, d), lambda i, j: (i, 0)),
            pl.BlockSpec((1, d), lambda i, j: (0, 0)),
            pl.BlockSpec((tn, d), lambda i, j: (j, 0)),
            pl.BlockSpec((SMALL_COLS, d), lambda i, j: (0, 0)),
        ],
        out_specs=[
            pl.BlockSpec((tm, tn), lambda i, j: (i, j)),
            pl.BlockSpec((tm, SMALL_COLS), lambda i, j: (i, 0)),
        ],
        out_shape=[
            jax.ShapeDtypeStruct((m, n_main), F32),
            jax.ShapeDtypeStruct((m, SMALL_COLS), F32),
        ],
        scratch_shapes=[pltpu.VMEM((tm, d), BF16)],
        compiler_params=pltpu.CompilerParams(
            dimension_semantics=("parallel", "arbitrary"),
            vmem_limit_bytes=_vmem_limit(blocks, tm * d * 2, tm * d * 4 + tm * tn * 4)),
        name="inproj",
    )(x, norm_w, w_in_t, w_small_t)


def _retention_log_gamma():
    return [math.log(1.0 - 2.0 ** (-5 - h)) for h in range(RET_HEADS)]


def _retention_setup(state_ref, decay_ref, qdec_ref, kdec_ref, *, tb, first_step, sequence_start):
    log_gamma = _retention_log_gamma()

    @pl.when(first_step)
    def _():
        ri = lax.broadcasted_iota(jnp.int32, (tb, tb), 0)
        ci = lax.broadcasted_iota(jnp.int32, (tb, tb), 1)
        causal = ri >= ci
        rel = jnp.where(causal, (ri - ci).astype(F32), 0.0)
        idx = lax.broadcasted_iota(jnp.int32, (tb, HEAD_DIM), 0).astype(F32)
        for h in range(RET_HEADS):
            decay_ref[h] = jnp.where(causal, jnp.exp(log_gamma[h] * rel), 0.0)
            qdec_ref[h] = jnp.exp(log_gamma[h] * (idx + 1.0))
            kdec_ref[h] = jnp.exp(log_gamma[h] * (tb - 1.0 - idx))

    @pl.when(sequence_start)
    def _():
        state_ref[...] = jnp.zeros_like(state_ref)


def _retention_step(pos_ref, invf_ref, sgn_ref, q_ref, k_ref, v_ref, g_ref, o_ref,
                    state_ref, decay_ref, qdec_ref, kdec_ref, *, tb):
    heads = range(RET_HEADS)
    lanes = lambda h: slice(h * HEAD_DIM, (h + 1) * HEAD_DIM)
    log_gamma = _retention_log_gamma()

    pos = pos_ref[0].astype(F32)
    pos_rows = jnp.broadcast_to(pos, (V7X_LANES, tb)).T
    ang = pos_rows * invf_ref[...]
    cos2 = jnp.cos(ang)
    sin2 = jnp.sin(ang) * sgn_ref[...]
    scale = HEAD_DIM ** -0.5

    rot = lambda x: x * cos2 + pltpu.roll(x, HEAD_DIM // 2, 1) * sin2
    qr = [rot(q_ref[:, lanes(h)]) for h in heads]
    kr = [rot(k_ref[:, lanes(h)]) * scale for h in heads]
    v = [v_ref[:, lanes(h)] for h in heads]
    state = [state_ref[h] for h in heads]
    scores = [_mm_nt(qr[h], kr[h]) for h in heads]
    inter = [_mm(qr[h] * qdec_ref[h], state[h]) for h in heads]
    kv = [_mm_tn(kr[h] * kdec_ref[h], v[h]) for h in heads]
    intra = [_mm(scores[h] * decay_ref[h], v[h]) for h in heads]
    for h in heads:
        state_ref[h] = math.exp(log_gamma[h] * tb) * state[h] + kv[h]
        o = intra[h] + inter[h]
        mu = jnp.mean(o, axis=-1, keepdims=True)
        oc = o - mu
        var = jnp.mean(oc * oc, axis=-1, keepdims=True)
        o_ref[:, lanes(h)] = (oc * lax.rsqrt(var + EPS) * _silu(g_ref[:, lanes(h)])).astype(BF16)


def _softplus(x):
    return jnp.maximum(x, 0.0) + jnp.log1p(jnp.exp(-jnp.abs(x)))


def _unit_lower_inverses(ms, eye, blk16, blk32):
    d = [jnp.where(blk16, m, 0.0) for m in ms]
    d2 = [_mm(a, a) for a in d]
    d4 = [_mm(a, a) for a in d2]
    d8 = [_mm(a, a) for a in d4]
    p = [_mm(eye - a, eye + b) for a, b in zip(d, d2)]
    p = [_mm(a, eye + b) for a, b in zip(p, d4)]
    p = [_mm(a, eye + b) for a, b in zip(p, d8)]
    off16 = blk32 & jnp.logical_not(blk16)
    lp = [_mm(jnp.where(off16, m, 0.0), a) for m, a in zip(ms, p)]
    p = [a - _mm(a, b) for a, b in zip(p, lp)]
    lp = [_mm(jnp.where(blk32, 0.0, m), a) for m, a in zip(ms, p)]
    p = [a - _mm(a, b) for a, b in zip(p, lp)]
    return p


def _gdn_setup(pad_ref, state_ref, *, sequence_start):
    @pl.when(sequence_start)
    def _():
        state_ref[...] = jnp.zeros_like(state_ref)
        pad_ref[:, 0:V7X_SUBLANES, :] = jnp.zeros((3, V7X_SUBLANES, GDN_WIDTH), F32)


def _gdn_step(q_ref, k_ref, v_ref, z_ref, s_ref, cw_ref, alog_ref, dtb_ref, nw_ref,
              o_ref, pad_ref, conv_ref, state_ref, *, tb):
    c_len = GDN_CHUNK
    nc = tb // c_len
    halo = V7X_SUBLANES

    groups = [(0, halo)] + [(lo, min(CONV_ROWS, tb - lo)) for lo in range(halo, tb, CONV_ROWS)]
    for a, ref in enumerate((q_ref, k_ref, v_ref)):
        w = cw_ref[:, a * GDN_WIDTH:(a + 1) * GDN_WIDTH]
        taps = [w[CONV_K - 1 - back:CONV_K - back, :] for back in range(CONV_K)]
        pad_ref[a, halo:2 * halo, :] = ref[0:halo, :]
        for lo, n in groups:
            acc = ref[lo:lo + n, :] * taps[0]
            for back in range(1, CONV_K):
                prev = pad_ref[a, halo - back:2 * halo - back, :] if lo == 0 else ref[lo - back:lo - back + n, :]
                acc = acc + prev * taps[back]
            conv_ref[a, lo:lo + n, :] = _silu(acc)
        pad_ref[a, 0:halo, :] = ref[tb - halo:tb, :]

    small = s_ref[...]
    lane = lax.broadcasted_iota(jnp.int32, (tb, SMALL_COLS), 1)
    g_all = jnp.where(lane < GDN_HEADS, -jnp.exp(alog_ref[...]) * _softplus(small + dtb_ref[...]), 0.0)
    beta_all = jax.nn.sigmoid(small)

    ri = lax.broadcasted_iota(jnp.int32, (tb, tb), 0)
    ci = lax.broadcasted_iota(jnp.int32, (tb, tb), 1)
    chunk_tri = ((ri >= ci) & ((ri >> 6) == (ci >> 6))).astype(F32)
    gc = jnp.dot(chunk_tri, g_all, precision=lax.Precision.HIGHEST, preferred_element_type=F32)
    gct = gc.T

    r64 = lax.broadcasted_iota(jnp.int32, (c_len, c_len), 0)
    c64 = lax.broadcasted_iota(jnp.int32, (c_len, c_len), 1)
    causal = r64 >= c64
    strict = r64 > c64
    eye = (r64 == c64).astype(F32)
    blk16 = (r64 >> 4) == (c64 >> 4)
    blk32 = (r64 >> 5) == (c64 >> 5)
    scale = HEAD_DIM ** -0.5

    heads = range(GDN_HEADS)
    chains = [(h, c) for h in heads for c in range(nc)]
    rows = lambda c: slice(c * c_len, (c + 1) * c_len)
    lanes = lambda h: slice(h * HEAD_DIM, (h + 1) * HEAD_DIM)

    qn, kn, kb, vb, kbe, qe, g_b = [], [], [], [], [], [], []
    for h in heads:
        qh, kh, vh = conv_ref[0, :, lanes(h)], conv_ref[1, :, lanes(h)], conv_ref[2, :, lanes(h)]
        g_h = jnp.broadcast_to(gc[:, h:h + 1], (tb, HEAD_DIM))
        b_h = jnp.broadcast_to(beta_all[:, GDN_HEADS + h:GDN_HEADS + h + 1], (tb, HEAD_DIM))
        e_h = jnp.exp(g_h)
        q_h = qh * lax.rsqrt(jnp.sum(qh * qh, axis=-1, keepdims=True) + EPS) * scale
        k_h = kh * lax.rsqrt(jnp.sum(kh * kh, axis=-1, keepdims=True) + EPS)
        kb_h = k_h * b_h
        qn.append(q_h), kn.append(k_h), kb.append(kb_h), vb.append(vh * b_h)
        kbe.append(kb_h * e_h), qe.append(q_h * e_h), g_b.append(g_h)

    decay = [jnp.exp(jnp.where(causal, g_b[h][rows(c), :c_len] - gct[h:h + 1, rows(c)], -jnp.inf))
             for h, c in chains]
    kq = [_mm_nt(jnp.concatenate([kb[h][rows(c)], qn[h][rows(c)]], axis=0), kn[h][rows(c)])
          for h, c in chains]
    ms = [jnp.where(strict, a[:c_len] * dc, 0.0) for a, dc in zip(kq, decay)]
    attn = [a[c_len:] * dc for a, dc in zip(kq, decay)]
    t_inv = _unit_lower_inverses(ms, eye, blk16, blk32)
    rhs = [jnp.concatenate([vb[h][rows(c)], kbe[h][rows(c)]], axis=1) for h, c in chains]
    sol = [r + _mm(t - eye, r) for t, r in zip(t_inv, rhs)]

    state = [state_ref[h] for h in heads]
    outs = [[] for _ in heads]
    for c in range(nc):
        last = (c + 1) * c_len - 1
        g_last = [g_b[h][last:last + 1, :] for h in heads]
        k_dec = [kn[h][rows(c)] * jnp.exp(g_last[h] - g_b[h][rows(c)]) for h in heads]
        w_s = [_mm(sol[h * nc + c][:, HEAD_DIM:], state[h]) for h in heads]
        q_s = [_mm(qe[h][rows(c)], state[h]) for h in heads]
        v_new = [sol[h * nc + c][:, :HEAD_DIM] - w_s[h] for h in heads]
        a_v = [_mm(attn[h * nc + c], v_new[h]) for h in heads]
        k_v = [_mm_tn(k_dec[h], v_new[h]) for h in heads]
        state = [state[h] * jnp.exp(g_last[h]) + k_v[h] for h in heads]
        for h in heads:
            outs[h].append(q_s[h] + a_v[h])

    for h in heads:
        state_ref[h] = state[h]
        o = jnp.concatenate(outs[h], axis=0)
        o = o * lax.rsqrt(jnp.mean(o * o, axis=-1, keepdims=True) + EPS) * nw_ref[...]
        o_ref[:, lanes(h)] = (o * _silu(z_ref[:, lanes(h)])).astype(BF16)


def _mixer_kernel(pos_ref, invf_ref, sgn_ref, rq_ref, rk_ref, rv_ref, rg_ref,
                  gq_ref, gk_ref, gv_ref, gz_ref, s_ref, cw_ref, alog_ref, dtb_ref, nw_ref,
                  x_ref, wr_ref, wg_ref, o_ref,
                  r_state, r_decay, r_qdec, r_kdec, g_pad, g_conv, g_state, ro_ref, go_ref, *, tb, nt):
    s = pl.program_id(0)
    sequence_start = s % nt == 0

    @pl.when(s == 0)
    def _():
        ro_ref[...] = jnp.zeros_like(ro_ref)
        go_ref[...] = jnp.zeros_like(go_ref)

    _gdn_setup(g_pad, g_state, sequence_start=sequence_start)
    _retention_setup(r_state, r_decay, r_qdec, r_kdec, tb=tb, first_step=s == 0, sequence_start=sequence_start)
    o_ref[...] = (x_ref[...]
                  + jnp.dot(ro_ref[...], wr_ref[...], preferred_element_type=F32)
                  + jnp.dot(go_ref[...], wg_ref[...], preferred_element_type=F32))
    _gdn_step(gq_ref, gk_ref, gv_ref, gz_ref, s_ref, cw_ref, alog_ref, dtb_ref, nw_ref,
              go_ref, g_pad, g_conv, g_state, tb=tb)
    _retention_step(pos_ref, invf_ref, sgn_ref, rq_ref, rk_ref, rv_ref, rg_ref, ro_ref,
                    r_state, r_decay, r_qdec, r_kdec, tb=tb)


def _mixer(x, proj, small, pos3, invf2, sgn2, conv_w, alog_row, dtb_row, norm_w, w_out, batch, seq):
    assert RET_TB == GDN_TB
    tb = GDN_TB
    nt = seq // tb
    nblk = batch * nt
    m, d = x.shape
    cur = lambda s: jnp.minimum(s, nblk - 1)
    lag = lambda s: jnp.maximum(s - 1, 0)
    col = lambda c: pl.BlockSpec((tb, RET_WIDTH), lambda s, c=c: (cur(s), c))
    row = lambda n: pl.BlockSpec((1, n), lambda s: (0, 0))
    once = pl.Buffered(1)
    blocks = (8 * tb * GDN_WIDTH * 4 + tb * SMALL_COLS * 4 + CONV_K * 3 * GDN_WIDTH * 4
              + 2 * tb * d * 4 + tb * 4 + 5 * V7X_LANES * 4)
    scratch = (RET_HEADS * (HEAD_DIM * HEAD_DIM + tb * tb + 2 * tb * HEAD_DIM) * 4
               + 3 * 2 * V7X_SUBLANES * GDN_WIDTH * 4 + 3 * tb * GDN_WIDTH * 4
               + GDN_HEADS * HEAD_DIM * HEAD_DIM * 4 + 2 * tb * GDN_WIDTH * 2
               + (RET_WIDTH + GDN_WIDTH) * d * 2)
    return pl.pallas_call(
        functools.partial(_mixer_kernel, tb=tb, nt=nt),
        grid=(nblk + 1,),
        in_specs=[
            pl.BlockSpec((1, 1, tb), lambda s: (cur(s), 0, 0)), row(V7X_LANES), row(V7X_LANES),
            col(0), col(1), col(2), col(3), col(4), col(5), col(6), col(7),
            pl.BlockSpec((tb, SMALL_COLS), lambda s: (cur(s), 0)),
            pl.BlockSpec((CONV_K, 3 * GDN_WIDTH), lambda s: (0, 0)),
            row(SMALL_COLS), row(SMALL_COLS), row(HEAD_DIM),
            pl.BlockSpec((tb, d), lambda s: (lag(s), 0)),
            pl.BlockSpec((RET_WIDTH, d), lambda s: (0, 0), pipeline_mode=once),
            pl.BlockSpec((GDN_WIDTH, d), lambda s: (RET_WIDTH // GDN_WIDTH, 0), pipeline_mode=once),
        ],
        out_specs=pl.BlockSpec((tb, d), lambda s: (lag(s), 0)),
        out_shape=jax.ShapeDtypeStruct((m, d), F32),
        scratch_shapes=[
            pltpu.VMEM((RET_HEADS, HEAD_DIM, HEAD_DIM), F32),
            pltpu.VMEM((RET_HEADS, tb, tb), F32),
            pltpu.VMEM((RET_HEADS, tb, HEAD_DIM), F32),
            pltpu.VMEM((RET_HEADS, tb, HEAD_DIM), F32),
            pltpu.VMEM((3, 2 * V7X_SUBLANES, GDN_WIDTH), F32),
            pltpu.VMEM((3, tb, GDN_WIDTH), F32),
            pltpu.VMEM((GDN_HEADS, HEAD_DIM, HEAD_DIM), F32),
            pltpu.VMEM((tb, RET_WIDTH), BF16),
            pltpu.VMEM((tb, GDN_WIDTH), BF16),
        ],
        compiler_params=pltpu.CompilerParams(
            dimension_semantics=("arbitrary",),
            vmem_limit_bytes=_vmem_limit(blocks, scratch, 16 * tb * tb * 4 + 20 * tb * GDN_WIDTH * 4)),
        name="mixer",
    )(pos3, invf2, sgn2, *([proj] * 8), small, conv_w, alog_row, dtb_row, norm_w, x, w_out, w_out)


def kernel(x, positions, norm_ffn1_w, ffn1_w_gate, ffn1_w_up, ffn1_w_down, norm_mix_w, w_in, conv_w,
           gdn_a_log, gdn_dt_bias, gdn_norm_w, w_out, norm_ffn2_w, ffn2_w_gate, ffn2_w_up, ffn2_w_down,
           norm_final_w):
    batch, seq, d = x.shape
    depth = norm_ffn1_w.shape[0]
    m = batch * seq
    n_main = 4 * RET_WIDTH + 4 * GDN_WIDTH
    row = lambda v: v.reshape(1, -1).astype(F32)

    half = HEAD_DIM // 2
    inv_freq = ROPE_THETA ** (-jnp.arange(half, dtype=F32) / half)
    invf2 = jnp.concatenate([inv_freq, inv_freq]).reshape(1, HEAD_DIM)
    sgn2 = jnp.concatenate([-jnp.ones((half,), F32), jnp.ones((half,), F32)]).reshape(1, HEAD_DIM)
    pos3 = positions.reshape(m // RET_TB, 1, RET_TB)
    pad_heads = lambda v: jnp.pad(v.astype(F32), (0, SMALL_COLS - GDN_HEADS)).reshape(1, SMALL_COLS)

    xf = x.reshape(m, d)
    for l in range(depth):
        last = l == depth - 1
        w_in_t = jnp.swapaxes(w_in[l], 0, 1)
        later = ((w_in_t, n_main), (w_out[l], w_out.shape[1]), (ffn2_w_gate[l], d), (ffn2_w_up[l], d),
                 (ffn2_w_down[l], ffn2_w_down.shape[1]))
        head, wg1, wu1, wd1 = _ffn_head(xf, row(norm_ffn1_w[l]), ffn1_w_gate[l], ffn1_w_up[l], ffn1_w_down[l])
        xf, (w_in_b, w_out_b, wg2, wu2, wd2) = _ffn(
            xf, row(norm_ffn1_w[l]), wg1, wu1, wd1, row(norm_final_w), later, final_norm=False, done=(head, 1))
        w_small = jnp.pad(w_in_t[n_main:], ((0, SMALL_COLS - 2 * GDN_HEADS), (0, 0))).astype(BF16)
        proj, small = _inproj(xf, row(norm_mix_w[l]), w_in_b, w_small, n_main)
        xf = _mixer(xf, proj, small, pos3, invf2, sgn2, conv_w[l].astype(F32), pad_heads(gdn_a_log[l]),
                    pad_heads(gdn_dt_bias[l]), row(gdn_norm_w[l]), w_out_b, batch, seq)
        xf, _ = _ffn(xf, row(norm_ffn2_w[l]), wg2, wu2, wd2, row(norm_final_w), final_norm=last)
    return xf.reshape(batch, seq, d)
```

```python
import functools
import math

import jax
import jax.numpy as jnp
from jax import lax
from jax.experimental import pallas as pl
from jax.experimental.pallas import tpu as pltpu

F32 = jnp.float32
BF16 = jnp.bfloat16

D_MODEL = 2048
HEAD_DIM = 128
RET_HEADS = 8
GDN_HEADS = 8
RET_WIDTH = RET_HEADS * HEAD_DIM
GDN_WIDTH = GDN_HEADS * HEAD_DIM
D_FF = 5632
GDN_CHUNK = 64
CONV_K = 4
ROPE_THETA = 10000.0
EPS = 1e-6

V7X_LANES = 128
V7X_SUBLANES = 8
V7X_BF16_SUBLANES = 16
V7X_VMEM_BYTES = 64 * 2**20

FFN_TM, FFN_TF = 1024, 512
FFN_HEAD_TF = 256
INPROJ_TM, INPROJ_TN = 1024, 2048
OUTPROJ_TM = 512
RET_TB = 256
GDN_TB = 256
CONV_ROWS = 32
SMALL_COLS = V7X_LANES


def _vmem_limit(pipelined_bytes, scratch_bytes, temp_bytes):
    need = 2 * pipelined_bytes + scratch_bytes + temp_bytes
    return int(min(need, V7X_VMEM_BYTES * 15 // 16))


def _mm(a, b):
    return jnp.dot(a.astype(BF16), b.astype(BF16), preferred_element_type=F32)


def _mm_nt(a, b):
    return lax.dot_general(a.astype(BF16), b.astype(BF16), (((1,), (1,)), ((), ())),
                           preferred_element_type=F32)


def _mm_tn(a, b):
    return lax.dot_general(a.astype(BF16), b.astype(BF16), (((0,), (0,)), ((), ())),
                           preferred_element_type=F32)


def _rmsnorm(x, w):
    return x * lax.rsqrt(jnp.mean(x * x, axis=-1, keepdims=True) + EPS) * w


def _silu(x):
    return x * jax.nn.sigmoid(x)


def _ffn_kernel(x_ref, nw_ref, wg_ref, wu_ref, wd_ref, fw_ref, *rest, final_norm, n_cast):
    cast_src = rest[:n_cast]
    o_ref = rest[n_cast]
    cast_dst = rest[n_cast + 1:2 * n_cast + 1]
    h_ref, act_ref = rest[2 * n_cast + 1:2 * n_cast + 3]
    j = pl.program_id(1)
    n_tiles = pl.num_programs(1) - 1

    def gate_up():
        h = h_ref[...]
        g = jnp.dot(h, wg_ref[...], preferred_element_type=F32)
        u = jnp.dot(h, wu_ref[...], preferred_element_type=F32)
        act_ref[...] = ((_silu(g) * u) * 0.5).astype(BF16)

    def down():
        return jnp.dot(act_ref[...], wd_ref[...], preferred_element_type=F32)

    @pl.when(j == 0)
    def _():
        x = x_ref[...]
        h_ref[...] = _rmsnorm(x, nw_ref[...]).astype(BF16)
        o_ref[...] = x
        gate_up()

    @pl.when((j > 0) & (j < n_tiles))
    def _():
        o_ref[...] += down()
        gate_up()

    @pl.when(j == n_tiles)
    def _():
        o = o_ref[...] + down()
        o_ref[...] = _rmsnorm(o, fw_ref[...]) if final_norm else o

    for src, dst in zip(cast_src, cast_dst):
        dst[...] = src[...].astype(BF16)


def _cast_plan(w, rows, n_steps):
    slab = V7X_BF16_SUBLANES * pl.cdiv(rows, V7X_BF16_SUBLANES * n_steps)
    while rows % slab:
        slab += V7X_BF16_SUBLANES
    return w, slab, rows // slab


def _ffn(x, norm_w, wg, wu, wd, final_w, cast_weights=(), *, final_norm, done=None):
    m, d = x.shape
    f = wg.shape[1]
    tm, tf = FFN_TM, FFN_TF
    skip = 0 if done is None else done[1]
    ni, n_tiles = m // tm - skip, f // tf
    nj = n_tiles + 1
    plans = [_cast_plan(w, rows, ni * nj) for w, rows in cast_weights]
    cast_specs = [pl.BlockSpec((r, v.shape[1]), lambda i, j, used=used: (jnp.minimum(i * nj + j, used - 1), 0))
                  for v, r, used in plans]
    cast_bytes = sum(r * v.shape[1] * (4 + 2) for v, r, _ in plans)
    blocks = 2 * tm * d * 4 + 3 * d * tf * 2 + 2 * d * 4 + cast_bytes
    operands = [x, norm_w, wg, wu, wd, final_w] + [v for v, _, _ in plans]
    in_specs = [
        pl.BlockSpec((tm, d), lambda i, j: (i + skip, 0)),
        pl.BlockSpec((1, d), lambda i, j: (0, 0)),
        pl.BlockSpec((d, tf), lambda i, j: (0, jnp.minimum(j, n_tiles - 1))),
        pl.BlockSpec((d, tf), lambda i, j: (0, jnp.minimum(j, n_tiles - 1))),
        pl.BlockSpec((tf, d), lambda i, j: (jnp.maximum(j - 1, 0), 0)),
        pl.BlockSpec((1, d), lambda i, j: (0, 0)),
    ] + cast_specs
    body = functools.partial(_ffn_kernel, final_norm=final_norm, n_cast=len(plans))
    aliases = {}
    if done is not None:
        operands.append(done[0])
        in_specs.append(pl.BlockSpec(memory_space=pl.ANY))
        aliases = {len(operands) - 1: 0}
        body = functools.partial(_drop_operand, body, len(operands) - 1)
    outs = pl.pallas_call(
        body,
        grid=(ni, nj),
        in_specs=in_specs,
        out_specs=[pl.BlockSpec((tm, d), lambda i, j: (i + skip, 0))] + cast_specs,
        out_shape=[jax.ShapeDtypeStruct((m, d), F32)]
        + [jax.ShapeDtypeStruct((used * r, v.shape[1]), BF16) for v, r, used in plans],
        scratch_shapes=[pltpu.VMEM((tm, d), BF16), pltpu.VMEM((tm, tf), BF16)],
        input_output_aliases=aliases,
        compiler_params=pltpu.CompilerParams(
            dimension_semantics=("arbitrary", "arbitrary"),
            vmem_limit_bytes=_vmem_limit(blocks, tm * d * 2 + tm * tf * 2, 4 * tm * tf * 4)),
        name="ffn_final" if final_norm else "ffn",
    )(*operands)
    return outs[0], list(outs[1:])


def _drop_operand(body, index, *refs, **kwargs):
    return body(*refs[:index], *refs[index + 1:], **kwargs)


def _ffn_head_kernel(x_ref, nw_ref, wg_ref, wu_ref, wd_ref, o_ref, wgb_ref, wub_ref, wdb_ref, h_ref):
    j = pl.program_id(0)

    @pl.when(j == 0)
    def _():
        x = x_ref[...]
        h_ref[...] = _rmsnorm(x, nw_ref[...]).astype(BF16)
        o_ref[...] = x

    wg = wg_ref[...].astype(BF16)
    wu = wu_ref[...].astype(BF16)
    wd = wd_ref[...].astype(BF16)
    wgb_ref[...] = wg
    wub_ref[...] = wu
    wdb_ref[...] = wd
    h = h_ref[...]
    g = jnp.dot(h, wg, preferred_element_type=F32)
    u = jnp.dot(h, wu, preferred_element_type=F32)
    act = (_silu(g) * u) * 0.5
    o_ref[...] += jnp.dot(act.astype(BF16), wd, preferred_element_type=F32)


def _ffn_head(x, norm_w, wg, wu, wd):
    m, d = x.shape
    f = wg.shape[1]
    tm, tf = FFN_TM, FFN_HEAD_TF
    blocks = 2 * tm * d * 4 + 3 * d * tf * (4 + 2) + d * 4
    return pl.pallas_call(
        _ffn_head_kernel,
        grid=(f // tf,),
        in_specs=[
            pl.BlockSpec((tm, d), lambda j: (0, 0)),
            pl.BlockSpec((1, d), lambda j: (0, 0)),
            pl.BlockSpec((d, tf), lambda j: (0, j)),
            pl.BlockSpec((d, tf), lambda j: (0, j)),
            pl.BlockSpec((tf, d), lambda j: (j, 0)),
        ],
        out_specs=[
            pl.BlockSpec((tm, d), lambda j: (0, 0)),
            pl.BlockSpec((d, tf), lambda j: (0, j)),
            pl.BlockSpec((d, tf), lambda j: (0, j)),
            pl.BlockSpec((tf, d), lambda j: (j, 0)),
        ],
        out_shape=[
            jax.ShapeDtypeStruct((m, d), F32),
            jax.ShapeDtypeStruct((d, f), BF16),
            jax.ShapeDtypeStruct((d, f), BF16),
            jax.ShapeDtypeStruct((f, d), BF16),
        ],
        scratch_shapes=[pltpu.VMEM((tm, d), BF16)],
        compiler_params=pltpu.CompilerParams(
            dimension_semantics=("arbitrary",),
            vmem_limit_bytes=_vmem_limit(blocks, tm * d * 2, 4 * tm * tf * 4 + 3 * d * tf * 2)),
        name="ffn_head",
    )(x, norm_w, wg, wu, wd)


def _inproj_kernel(x_ref, nw_ref, wt_ref, wst_ref, o_ref, os_ref, h_ref):
    j = pl.program_id(1)

    @pl.when(j == 0)
    def _():
        hb = _rmsnorm(x_ref[...], nw_ref[...]).astype(BF16)
        h_ref[...] = hb
        os_ref[...] = lax.dot_general(hb, wst_ref[...], (((1,), (1,)), ((), ())), preferred_element_type=F32)

    o_ref[...] = lax.dot_general(h_ref[...], wt_ref[...], (((1,), (1,)), ((), ())), preferred_element_type=F32)


def _inproj(x, norm_w, w_in_t, w_small_t, n_main):
    m, d = x.shape
    tm, tn = INPROJ_TM, INPROJ_TN
    blocks = tm * d * 4 + d * tn * 2 + d * SMALL_COLS * 2 + tm * tn * 4 + tm * SMALL_COLS * 4 + d * 4
    return pl.pallas_call(
        _inproj_kernel,
        grid=(m // tm, n_main // tn),
        in_specs=[
            pl.BlockSpec((tm, d), lambda i, j: (i, 0)),
            pl.BlockSpec((1, d), lambda i, j: (0, 0)),
            pl.BlockSpec((tn, d), lambda i, j: (j, 0)),
            pl.BlockSpec((SMALL_COLS, d), lambda i, j: (0, 0)),
        ],
        out_specs=[
            pl.BlockSpec((tm, tn), lambda i, j: (i, j)),
            pl.BlockSpec((tm, SMALL_COLS), lambda i, j: (i, 0)),
        ],
        out_shape=[
            jax.ShapeDtypeStruct((m, n_main), F32),
            jax.ShapeDtypeStruct((m, SMALL_COLS), F32),
        ],
        scratch_shapes=[pltpu.VMEM((tm, d), BF16)],
        compiler_params=pltpu.CompilerParams(
            dimension_semantics=("parallel", "arbitrary"),
            vmem_limit_bytes=_vmem_limit(blocks, tm * d * 2, tm * d * 4 + tm * tn * 4)),
        name="inproj",
    )(x, norm_w, w_in_t, w_small_t)


def _retention_log_gamma():
    return [math.log(1.0 - 2.0 ** (-5 - h)) for h in range(RET_HEADS)]


def _retention_setup(state_ref, decay_ref, qdec_ref, kdec_ref, *, tb):
    log_gamma = _retention_log_gamma()

    @pl.when((pl.program_id(0) == 0) & (pl.program_id(1) == 0))
    def _():
        ri = lax.broadcasted_iota(jnp.int32, (tb, tb), 0)
        ci = lax.broadcasted_iota(jnp.int32, (tb, tb), 1)
        causal = ri >= ci
        rel = jnp.where(causal, (ri - ci).astype(F32), 0.0)
        idx = lax.broadcasted_iota(jnp.int32, (tb, HEAD_DIM), 0).astype(F32)
        for h in range(RET_HEADS):
            decay_ref[h] = jnp.where(causal, jnp.exp(log_gamma[h] * rel), 0.0)
            qdec_ref[h] = jnp.exp(log_gamma[h] * (idx + 1.0))
            kdec_ref[h] = jnp.exp(log_gamma[h] * (tb - 1.0 - idx))

    @pl.when(pl.program_id(1) == 0)
    def _():
        state_ref[...] = jnp.zeros_like(state_ref)


def _retention_step(pos_ref, invf_ref, sgn_ref, q_ref, k_ref, v_ref, g_ref, o_ref,
                    state_ref, decay_ref, qdec_ref, kdec_ref, *, tb):
    heads = range(RET_HEADS)
    lanes = lambda h: slice(h * HEAD_DIM, (h + 1) * HEAD_DIM)
    log_gamma = _retention_log_gamma()

    pos = pos_ref[0].astype(F32)
    pos_rows = jnp.broadcast_to(pos, (V7X_LANES, tb)).T
    ang = pos_rows * invf_ref[...]
    cos2 = jnp.cos(ang)
    sin2 = jnp.sin(ang) * sgn_ref[...]
    scale = HEAD_DIM ** -0.5

    rot = lambda x: x * cos2 + pltpu.roll(x, HEAD_DIM // 2, 1) * sin2
    qr = [rot(q_ref[:, lanes(h)]) for h in heads]
    kr = [rot(k_ref[:, lanes(h)]) * scale for h in heads]
    v = [v_ref[:, lanes(h)] for h in heads]
    state = [state_ref[h] for h in heads]
    scores = [_mm_nt(qr[h], kr[h]) for h in heads]
    inter = [_mm(qr[h] * qdec_ref[h], state[h]) for h in heads]
    kv = [_mm_tn(kr[h] * kdec_ref[h], v[h]) for h in heads]
    intra = [_mm(scores[h] * decay_ref[h], v[h]) for h in heads]
    for h in heads:
        state_ref[h] = math.exp(log_gamma[h] * tb) * state[h] + kv[h]
        o = intra[h] + inter[h]
        mu = jnp.mean(o, axis=-1, keepdims=True)
        oc = o - mu
        var = jnp.mean(oc * oc, axis=-1, keepdims=True)
        o_ref[:, lanes(h)] = (oc * lax.rsqrt(var + EPS) * _silu(g_ref[:, lanes(h)])).astype(BF16)


def _softplus(x):
    return jnp.maximum(x, 0.0) + jnp.log1p(jnp.exp(-jnp.abs(x)))


def _unit_lower_inverses(ms, eye, blk16, blk32):
    d = [jnp.where(blk16, m, 0.0) for m in ms]
    d2 = [_mm(a, a) for a in d]
    d4 = [_mm(a, a) for a in d2]
    d8 = [_mm(a, a) for a in d4]
    p = [_mm(eye - a, eye + b) for a, b in zip(d, d2)]
    p = [_mm(a, eye + b) for a, b in zip(p, d4)]
    p = [_mm(a, eye + b) for a, b in zip(p, d8)]
    off16 = blk32 & jnp.logical_not(blk16)
    lp = [_mm(jnp.where(off16, m, 0.0), a) for m, a in zip(ms, p)]
    p = [a - _mm(a, b) for a, b in zip(p, lp)]
    lp = [_mm(jnp.where(blk32, 0.0, m), a) for m, a in zip(ms, p)]
    p = [a - _mm(a, b) for a, b in zip(p, lp)]
    return p


def _gdn_setup(pad_ref, state_ref):
    @pl.when(pl.program_id(1) == 0)
    def _():
        state_ref[...] = jnp.zeros_like(state_ref)
        pad_ref[:, 0:V7X_SUBLANES, :] = jnp.zeros((3, V7X_SUBLANES, GDN_WIDTH), F32)


def _gdn_step(q_ref, k_ref, v_ref, z_ref, s_ref, cw_ref, alog_ref, dtb_ref, nw_ref,
              o_ref, pad_ref, conv_ref, state_ref, *, tb):
    c_len = GDN_CHUNK
    nc = tb // c_len
    halo = V7X_SUBLANES

    groups = [(0, halo)] + [(lo, min(CONV_ROWS, tb - lo)) for lo in range(halo, tb, CONV_ROWS)]
    for a, ref in enumerate((q_ref, k_ref, v_ref)):
        w = cw_ref[:, a * GDN_WIDTH:(a + 1) * GDN_WIDTH]
        taps = [w[CONV_K - 1 - back:CONV_K - back, :] for back in range(CONV_K)]
        pad_ref[a, halo:2 * halo, :] = ref[0:halo, :]
        for lo, n in groups:
            acc = ref[lo:lo + n, :] * taps[0]
            for back in range(1, CONV_K):
                prev = pad_ref[a, halo - back:2 * halo - back, :] if lo == 0 else ref[lo - back:lo - back + n, :]
                acc = acc + prev * taps[back]
            conv_ref[a, lo:lo + n, :] = _silu(acc)
        pad_ref[a, 0:halo, :] = ref[tb - halo:tb, :]

    small = s_ref[...]
    lane = lax.broadcasted_iota(jnp.int32, (tb, SMALL_COLS), 1)
    g_all = jnp.where(lane < GDN_HEADS, -jnp.exp(alog_ref[...]) * _softplus(small + dtb_ref[...]), 0.0)
    beta_all = jax.nn.sigmoid(small)

    ri = lax.broadcasted_iota(jnp.int32, (tb, tb), 0)
    ci = lax.broadcasted_iota(jnp.int32, (tb, tb), 1)
    chunk_tri = ((ri >= ci) & ((ri >> 6) == (ci >> 6))).astype(F32)
    gc = jnp.dot(chunk_tri, g_all, precision=lax.Precision.HIGHEST, preferred_element_type=F32)
    gct = gc.T

    r64 = lax.broadcasted_iota(jnp.int32, (c_len, c_len), 0)
    c64 = lax.broadcasted_iota(jnp.int32, (c_len, c_len), 1)
    causal = r64 >= c64
    strict = r64 > c64
    eye = (r64 == c64).astype(F32)
    blk16 = (r64 >> 4) == (c64 >> 4)
    blk32 = (r64 >> 5) == (c64 >> 5)
    scale = HEAD_DIM ** -0.5

    heads = range(GDN_HEADS)
    chains = [(h, c) for h in heads for c in range(nc)]
    rows = lambda c: slice(c * c_len, (c + 1) * c_len)
    lanes = lambda h: slice(h * HEAD_DIM, (h + 1) * HEAD_DIM)

    qn, kn, kb, vb, kbe, qe, g_b = [], [], [], [], [], [], []
    for h in heads:
        qh, kh, vh = conv_ref[0, :, lanes(h)], conv_ref[1, :, lanes(h)], conv_ref[2, :, lanes(h)]
        g_h = jnp.broadcast_to(gc[:, h:h + 1], (tb, HEAD_DIM))
        b_h = jnp.broadcast_to(beta_all[:, GDN_HEADS + h:GDN_HEADS + h + 1], (tb, HEAD_DIM))
        e_h = jnp.exp(g_h)
        q_h = qh * lax.rsqrt(jnp.sum(qh * qh, axis=-1, keepdims=True) + EPS) * scale
        k_h = kh * lax.rsqrt(jnp.sum(kh * kh, axis=-1, keepdims=True) + EPS)
        kb_h = k_h * b_h
        qn.append(q_h), kn.append(k_h), kb.append(kb_h), vb.append(vh * b_h)
        kbe.append(kb_h * e_h), qe.append(q_h * e_h), g_b.append(g_h)

    decay = [jnp.exp(jnp.where(causal, g_b[h][rows(c), :c_len] - gct[h:h + 1, rows(c)], -jnp.inf))
             for h, c in chains]
    kq = [_mm_nt(jnp.concatenate([kb[h][rows(c)], qn[h][rows(c)]], axis=0), kn[h][rows(c)])
          for h, c in chains]
    ms = [jnp.where(strict, a[:c_len] * dc, 0.0) for a, dc in zip(kq, decay)]
    attn = [a[c_len:] * dc for a, dc in zip(kq, decay)]
    t_inv = _unit_lower_inverses(ms, eye, blk16, blk32)
    rhs = [jnp.concatenate([vb[h][rows(c)], kbe[h][rows(c)]], axis=1) for h, c in chains]
    sol = [r + _mm(t - eye, r) for t, r in zip(t_inv, rhs)]

    state = [state_ref[h] for h in heads]
    outs = [[] for _ in heads]
    for c in range(nc):
        last = (c + 1) * c_len - 1
        g_last = [g_b[h][last:last + 1, :] for h in heads]
        k_dec = [kn[h][rows(c)] * jnp.exp(g_last[h] - g_b[h][rows(c)]) for h in heads]
        w_s = [_mm(sol[h * nc + c][:, HEAD_DIM:], state[h]) for h in heads]
        q_s = [_mm(qe[h][rows(c)], state[h]) for h in heads]
        v_new = [sol[h * nc + c][:, :HEAD_DIM] - w_s[h] for h in heads]
        a_v = [_mm(attn[h * nc + c], v_new[h]) for h in heads]
        k_v = [_mm_tn(k_dec[h], v_new[h]) for h in heads]
        state = [state[h] * jnp.exp(g_last[h]) + k_v[h] for h in heads]
        for h in heads:
            outs[h].append(q_s[h] + a_v[h])

    for h in heads:
        state_ref[h] = state[h]
        o = jnp.concatenate(outs[h], axis=0)
        o = o * lax.rsqrt(jnp.mean(o * o, axis=-1, keepdims=True) + EPS) * nw_ref[...]
        o_ref[:, lanes(h)] = (o * _silu(z_ref[:, lanes(h)])).astype(BF16)


def _mixer_kernel(pos_ref, invf_ref, sgn_ref, rq_ref, rk_ref, rv_ref, rg_ref,
                  gq_ref, gk_ref, gv_ref, gz_ref, s_ref, cw_ref, alog_ref, dtb_ref, nw_ref,
                  ro_ref, go_ref, r_state, r_decay, r_qdec, r_kdec, g_pad, g_conv, g_state, *, tb):
    _gdn_setup(g_pad, g_state)
    _retention_setup(r_state, r_decay, r_qdec, r_kdec, tb=tb)
    _gdn_step(gq_ref, gk_ref, gv_ref, gz_ref, s_ref, cw_ref, alog_ref, dtb_ref, nw_ref,
              go_ref, g_pad, g_conv, g_state, tb=tb)
    _retention_step(pos_ref, invf_ref, sgn_ref, rq_ref, rk_ref, rv_ref, rg_ref, ro_ref,
                    r_state, r_decay, r_qdec, r_kdec, tb=tb)


def _mixer(proj, small, pos3, invf2, sgn2, conv_w, alog_row, dtb_row, norm_w, batch, seq):
    assert RET_TB == GDN_TB
    tb = GDN_TB
    nt = seq // tb
    m = batch * seq
    col = lambda c: pl.BlockSpec((tb, RET_WIDTH), lambda b, t, c=c: (b * nt + t, c))
    row = lambda n: pl.BlockSpec((1, n), lambda b, t: (0, 0))
    out = pl.BlockSpec((tb, GDN_WIDTH), lambda b, t: (b * nt + t, 0))
    blocks = (8 * tb * GDN_WIDTH * 4 + tb * SMALL_COLS * 4 + CONV_K * 3 * GDN_WIDTH * 4
              + 2 * tb * GDN_WIDTH * 2 + tb * 4 + 5 * V7X_LANES * 4)
    scratch = (RET_HEADS * (HEAD_DIM * HEAD_DIM + tb * tb + 2 * tb * HEAD_DIM) * 4
               + 3 * 2 * V7X_SUBLANES * GDN_WIDTH * 4 + 3 * tb * GDN_WIDTH * 4
               + GDN_HEADS * HEAD_DIM * HEAD_DIM * 4)
    return pl.pallas_call(
        functools.partial(_mixer_kernel, tb=tb),
        grid=(batch, nt),
        in_specs=[
            pl.BlockSpec((1, 1, tb), lambda b, t: (b * nt + t, 0, 0)), row(V7X_LANES), row(V7X_LANES),
            col(0), col(1), col(2), col(3), col(4), col(5), col(6), col(7),
            pl.BlockSpec((tb, SMALL_COLS), lambda b, t: (b * nt + t, 0)),
            pl.BlockSpec((CONV_K, 3 * GDN_WIDTH), lambda b, t: (0, 0)),
            row(SMALL_COLS), row(SMALL_COLS), row(HEAD_DIM),
        ],
        out_specs=[out, out],
        out_shape=[jax.ShapeDtypeStruct((m, RET_WIDTH), BF16), jax.ShapeDtypeStruct((m, GDN_WIDTH), BF16)],
        scratch_shapes=[
            pltpu.VMEM((RET_HEADS, HEAD_DIM, HEAD_DIM), F32),
            pltpu.VMEM((RET_HEADS, tb, tb), F32),
            pltpu.VMEM((RET_HEADS, tb, HEAD_DIM), F32),
            pltpu.VMEM((RET_HEADS, tb, HEAD_DIM), F32),
            pltpu.VMEM((3, 2 * V7X_SUBLANES, GDN_WIDTH), F32),
            pltpu.VMEM((3, tb, GDN_WIDTH), F32),
            pltpu.VMEM((GDN_HEADS, HEAD_DIM, HEAD_DIM), F32),
        ],
        compiler_params=pltpu.CompilerParams(
            dimension_semantics=("arbitrary", "arbitrary"),
            vmem_limit_bytes=_vmem_limit(blocks, scratch, 16 * tb * tb * 4 + 20 * tb * GDN_WIDTH * 4)),
        name="mixer",
    )(pos3, invf2, sgn2, *([proj] * 8), small, conv_w, alog_row, dtb_row, norm_w)


def _outproj_kernel(x_ref, ro_ref, go_ref, wr_ref, wg_ref, o_ref):
    o_ref[...] = (x_ref[...]
                  + jnp.dot(ro_ref[...], wr_ref[...], preferred_element_type=F32)
                  + jnp.dot(go_ref[...], wg_ref[...], preferred_element_type=F32))


def _outproj(x, ro, go, w_out):
    m, d = x.shape
    tm = OUTPROJ_TM
    blocks = 2 * tm * d * 4 + tm * (RET_WIDTH + GDN_WIDTH) * 2 + (RET_WIDTH + GDN_WIDTH) * d * 2
    return pl.pallas_call(
        _outproj_kernel,
        grid=(m // tm,),
        in_specs=[
            pl.BlockSpec((tm, d), lambda i: (i, 0)),
            pl.BlockSpec((tm, RET_WIDTH), lambda i: (i, 0)),
            pl.BlockSpec((tm, GDN_WIDTH), lambda i: (i, 0)),
            pl.BlockSpec((RET_WIDTH, d), lambda i: (0, 0)),
            pl.BlockSpec((GDN_WIDTH, d), lambda i: (RET_WIDTH // GDN_WIDTH, 0)),
        ],
        out_specs=pl.BlockSpec((tm, d), lambda i: (i, 0)),
        out_shape=jax.ShapeDtypeStruct((m, d), F32),
        compiler_params=pltpu.CompilerParams(
            dimension_semantics=("parallel",),
            vmem_limit_bytes=_vmem_limit(blocks, 0, 2 * tm * d * 4)),
        name="outproj",
    )(x, ro, go, w_out, w_out)


def kernel(x, positions, norm_ffn1_w, ffn1_w_gate, ffn1_w_up, ffn1_w_down, norm_mix_w, w_in, conv_w,
           gdn_a_log, gdn_dt_bias, gdn_norm_w, w_out, norm_ffn2_w, ffn2_w_gate, ffn2_w_up, ffn2_w_down,
           norm_final_w):
    batch, seq, d = x.shape
    depth = norm_ffn1_w.shape[0]
    m = batch * seq
    n_main = 4 * RET_WIDTH + 4 * GDN_WIDTH
    row = lambda v: v.reshape(1, -1).astype(F32)

    half = HEAD_DIM // 2
    inv_freq = ROPE_THETA ** (-jnp.arange(half, dtype=F32) / half)
    invf2 = jnp.concatenate([inv_freq, inv_freq]).reshape(1, HEAD_DIM)
    sgn2 = jnp.concatenate([-jnp.ones((half,), F32), jnp.ones((half,), F32)]).reshape(1, HEAD_DIM)
    pos3 = positions.reshape(m // RET_TB, 1, RET_TB)
    pad_heads = lambda v: jnp.pad(v.astype(F32), (0, SMALL_COLS - GDN_HEADS)).reshape(1, SMALL_COLS)

    xf = x.reshape(m, d)
    for l in range(depth):
        last = l == depth - 1
        w_in_t = jnp.swapaxes(w_in[l], 0, 1)
        later = ((w_in_t, n_main), (w_out[l], w_out.shape[1]), (ffn2_w_gate[l], d), (ffn2_w_up[l], d),
                 (ffn2_w_down[l], ffn2_w_down.shape[1]))
        head, wg1, wu1, wd1 = _ffn_head(xf, row(norm_ffn1_w[l]), ffn1_w_gate[l], ffn1_w_up[l], ffn1_w_down[l])
        xf, (w_in_b, w_out_b, wg2, wu2, wd2) = _ffn(
            xf, row(norm_ffn1_w[l]), wg1, wu1, wd1, row(norm_final_w), later, final_norm=False, done=(head, 1))
        w_small = jnp.pad(w_in_t[n_main:], ((0, SMALL_COLS - 2 * GDN_HEADS), (0, 0))).astype(BF16)
        proj, small = _inproj(xf, row(norm_mix_w[l]), w_in_b, w_small, n_main)
        ro, go = _mixer(proj, small, pos3, invf2, sgn2, conv_w[l].astype(F32), pad_heads(gdn_a_log[l]),
                        pad_heads(gdn_dt_bias[l]), row(gdn_norm_w[l]), batch, seq)
        xf = _outproj(xf, ro, go, w_out_b)
        xf, _ = _ffn(xf, row(norm_ffn2_w[l]), wg2, wu2, wd2, row(norm_final_w), final_norm=last)
    return xf.reshape(batch, seq, d)
```

```python
import functools
import math

import jax
import jax.numpy as jnp
from jax import lax
from jax.experimental import pallas as pl
from jax.experimental.pallas import tpu as pltpu

F32 = jnp.float32
BF16 = jnp.bfloat16

D_MODEL = 2048
HEAD_DIM = 128
RET_HEADS = 8
GDN_HEADS = 8
RET_WIDTH = RET_HEADS * HEAD_DIM
GDN_WIDTH = GDN_HEADS * HEAD_DIM
D_FF = 5632
GDN_CHUNK = 64
CONV_K = 4
ROPE_THETA = 10000.0
EPS = 1e-6

V7X_LANES = 128
V7X_SUBLANES = 8
V7X_BF16_SUBLANES = 16
V7X_VMEM_BYTES = 64 * 2**20

FFN_TM, FFN_TF = 1024, 512
FFN_HEAD_TF = 256
INPROJ_TM, INPROJ_TN = 1024, 2048
OUTPROJ_TM = 512
RET_TB = 256
GDN_TB = 256
CONV_ROWS = 32
SMALL_COLS = V7X_LANES


def _vmem_limit(pipelined_bytes, scratch_bytes, temp_bytes):
    need = 2 * pipelined_bytes + scratch_bytes + temp_bytes
    return int(min(need, V7X_VMEM_BYTES * 15 // 16))


def _mm(a, b):
    return jnp.dot(a.astype(BF16), b.astype(BF16), preferred_element_type=F32)


def _mm_nt(a, b):
    return lax.dot_general(a.astype(BF16), b.astype(BF16), (((1,), (1,)), ((), ())),
                           preferred_element_type=F32)


def _mm_tn(a, b):
    return lax.dot_general(a.astype(BF16), b.astype(BF16), (((0,), (0,)), ((), ())),
                           preferred_element_type=F32)


def _rmsnorm(x, w):
    return x * lax.rsqrt(jnp.mean(x * x, axis=-1, keepdims=True) + EPS) * w


def _silu(x):
    return x * jax.nn.sigmoid(x)


def _ffn_kernel(x_ref, nw_ref, wg_ref, wu_ref, wd_ref, fw_ref, *rest, final_norm, n_cast):
    cast_src = rest[:n_cast]
    o_ref = rest[n_cast]
    cast_dst = rest[n_cast + 1:2 * n_cast + 1]
    h_ref = rest[2 * n_cast + 1]
    j = pl.program_id(1)

    @pl.when(j == 0)
    def _():
        x = x_ref[...]
        h_ref[...] = _rmsnorm(x, nw_ref[...]).astype(BF16)
        o_ref[...] = x

    h = h_ref[...]
    g = jnp.dot(h, wg_ref[...], preferred_element_type=F32)
    u = jnp.dot(h, wu_ref[...], preferred_element_type=F32)
    act = (_silu(g) * u) * 0.5
    o_ref[...] += jnp.dot(act.astype(BF16), wd_ref[...], preferred_element_type=F32)
    for src, dst in zip(cast_src, cast_dst):
        dst[...] = src[...].astype(BF16)

    if final_norm:
        @pl.when(j == pl.num_programs(1) - 1)
        def _():
            o_ref[...] = _rmsnorm(o_ref[...], fw_ref[...])


def _cast_plan(w, rows, n_steps):
    slab = V7X_BF16_SUBLANES * pl.cdiv(rows, V7X_BF16_SUBLANES * n_steps)
    while rows % slab:
        slab += V7X_BF16_SUBLANES
    return w, slab, rows // slab


def _ffn(x, norm_w, wg, wu, wd, final_w, cast_weights=(), *, final_norm, done=None):
    m, d = x.shape
    f = wg.shape[1]
    tm, tf = FFN_TM, FFN_TF
    skip = 0 if done is None else done[1]
    ni, nj = m // tm - skip, f // tf
    plans = [_cast_plan(w, rows, ni * nj) for w, rows in cast_weights]
    cast_specs = [pl.BlockSpec((r, v.shape[1]), lambda i, j, used=used: (jnp.minimum(i * nj + j, used - 1), 0))
                  for v, r, used in plans]
    cast_bytes = sum(r * v.shape[1] * (4 + 2) for v, r, _ in plans)
    blocks = 2 * tm * d * 4 + 3 * d * tf * 2 + 2 * d * 4 + cast_bytes
    operands = [x, norm_w, wg, wu, wd, final_w] + [v for v, _, _ in plans]
    in_specs = [
        pl.BlockSpec((tm, d), lambda i, j: (i + skip, 0)),
        pl.BlockSpec((1, d), lambda i, j: (0, 0)),
        pl.BlockSpec((d, tf), lambda i, j: (0, j)),
        pl.BlockSpec((d, tf), lambda i, j: (0, j)),
        pl.BlockSpec((tf, d), lambda i, j: (j, 0)),
        pl.BlockSpec((1, d), lambda i, j: (0, 0)),
    ] + cast_specs
    body = functools.partial(_ffn_kernel, final_norm=final_norm, n_cast=len(plans))
    aliases = {}
    if done is not None:
        operands.append(done[0])
        in_specs.append(pl.BlockSpec(memory_space=pl.ANY))
        aliases = {len(operands) - 1: 0}
        body = functools.partial(_drop_operand, body, len(operands) - 1)
    outs = pl.pallas_call(
        body,
        grid=(ni, nj),
        in_specs=in_specs,
        out_specs=[pl.BlockSpec((tm, d), lambda i, j: (i + skip, 0))] + cast_specs,
        out_shape=[jax.ShapeDtypeStruct((m, d), F32)]
        + [jax.ShapeDtypeStruct((used * r, v.shape[1]), BF16) for v, r, used in plans],
        scratch_shapes=[pltpu.VMEM((tm, d), BF16)],
        input_output_aliases=aliases,
        compiler_params=pltpu.CompilerParams(
            dimension_semantics=("arbitrary", "arbitrary"),
            vmem_limit_bytes=_vmem_limit(blocks, tm * d * 2, 4 * tm * tf * 4)),
        name="ffn_final" if final_norm else "ffn",
    )(*operands)
    return outs[0], list(outs[1:])


def _drop_operand(body, index, *refs, **kwargs):
    return body(*refs[:index], *refs[index + 1:], **kwargs)


def _ffn_head_kernel(x_ref, nw_ref, wg_ref, wu_ref, wd_ref, o_ref, wgb_ref, wub_ref, wdb_ref, h_ref):
    j = pl.program_id(0)

    @pl.when(j == 0)
    def _():
        x = x_ref[...]
        h_ref[...] = _rmsnorm(x, nw_ref[...]).astype(BF16)
        o_ref[...] = x

    wg = wg_ref[...].astype(BF16)
    wu = wu_ref[...].astype(BF16)
    wd = wd_ref[...].astype(BF16)
    wgb_ref[...] = wg
    wub_ref[...] = wu
    wdb_ref[...] = wd
    h = h_ref[...]
    g = jnp.dot(h, wg, preferred_element_type=F32)
    u = jnp.dot(h, wu, preferred_element_type=F32)
    act = (_silu(g) * u) * 0.5
    o_ref[...] += jnp.dot(act.astype(BF16), wd, preferred_element_type=F32)


def _ffn_head(x, norm_w, wg, wu, wd):
    m, d = x.shape
    f = wg.shape[1]
    tm, tf = FFN_TM, FFN_HEAD_TF
    blocks = 2 * tm * d * 4 + 3 * d * tf * (4 + 2) + d * 4
    return pl.pallas_call(
        _ffn_head_kernel,
        grid=(f // tf,),
        in_specs=[
            pl.BlockSpec((tm, d), lambda j: (0, 0)),
            pl.BlockSpec((1, d), lambda j: (0, 0)),
            pl.BlockSpec((d, tf), lambda j: (0, j)),
            pl.BlockSpec((d, tf), lambda j: (0, j)),
            pl.BlockSpec((tf, d), lambda j: (j, 0)),
        ],
        out_specs=[
            pl.BlockSpec((tm, d), lambda j: (0, 0)),
            pl.BlockSpec((d, tf), lambda j: (0, j)),
            pl.BlockSpec((d, tf), lambda j: (0, j)),
            pl.BlockSpec((tf, d), lambda j: (j, 0)),
        ],
        out_shape=[
            jax.ShapeDtypeStruct((m, d), F32),
            jax.ShapeDtypeStruct((d, f), BF16),
            jax.ShapeDtypeStruct((d, f), BF16),
            jax.ShapeDtypeStruct((f, d), BF16),
        ],
        scratch_shapes=[pltpu.VMEM((tm, d), BF16)],
        compiler_params=pltpu.CompilerParams(
            dimension_semantics=("arbitrary",),
            vmem_limit_bytes=_vmem_limit(blocks, tm * d * 2, 4 * tm * tf * 4 + 3 * d * tf * 2)),
        name="ffn_head",
    )(x, norm_w, wg, wu, wd)


def _inproj_kernel(x_ref, nw_ref, wt_ref, wst_ref, o_ref, os_ref, h_ref):
    j = pl.program_id(1)

    @pl.when(j == 0)
    def _():
        hb = _rmsnorm(x_ref[...], nw_ref[...]).astype(BF16)
        h_ref[...] = hb
        os_ref[...] = lax.dot_general(hb, wst_ref[...], (((1,), (1,)), ((), ())), preferred_element_type=F32)

    o_ref[...] = lax.dot_general(h_ref[...], wt_ref[...], (((1,), (1,)), ((), ())), preferred_element_type=F32)


def _inproj(x, norm_w, w_in_t, w_small_t, n_main):
    m, d = x.shape
    tm, tn = INPROJ_TM, INPROJ_TN
    blocks = tm * d * 4 + d * tn * 2 + d * SMALL_COLS * 2 + tm * tn * 4 + tm * SMALL_COLS * 4 + d * 4
    return pl.pallas_call(
        _inproj_kernel,
        grid=(m // tm, n_main // tn),
        in_specs=[
            pl.BlockSpec((tm, d), lambda i, j: (i, 0)),
            pl.BlockSpec((1, d), lambda i, j: (0, 0)),
            pl.BlockSpec((tn, d), lambda i, j: (j, 0)),
            pl.BlockSpec((SMALL_COLS, d), lambda i, j: (0, 0)),
        ],
        out_specs=[
            pl.BlockSpec((tm, tn), lambda i, j: (i, j)),
            pl.BlockSpec((tm, SMALL_COLS), lambda i, j: (i, 0)),
        ],
        out_shape=[
            jax.ShapeDtypeStruct((m, n_main), F32),
            jax.ShapeDtypeStruct((m, SMALL_COLS), F32),
        ],
        scratch_shapes=[pltpu.VMEM((tm, d), BF16)],
        compiler_params=pltpu.CompilerParams(
            dimension_semantics=("parallel", "arbitrary"),
            vmem_limit_bytes=_vmem_limit(blocks, tm * d * 2, tm * d * 4 + tm * tn * 4)),
        name="inproj",
    )(x, norm_w, w_in_t, w_small_t)


def _retention_log_gamma():
    return [math.log(1.0 - 2.0 ** (-5 - h)) for h in range(RET_HEADS)]


def _retention_setup(state_ref, decay_ref, qdec_ref, kdec_ref, *, tb):
    log_gamma = _retention_log_gamma()

    @pl.when((pl.program_id(0) == 0) & (pl.program_id(1) == 0))
    def _():
        ri = lax.broadcasted_iota(jnp.int32, (tb, tb), 0)
        ci = lax.broadcasted_iota(jnp.int32, (tb, tb), 1)
        causal = ri >= ci
        rel = jnp.where(causal, (ri - ci).astype(F32), 0.0)
        idx = lax.broadcasted_iota(jnp.int32, (tb, HEAD_DIM), 0).astype(F32)
        for h in range(RET_HEADS):
            decay_ref[h] = jnp.where(causal, jnp.exp(log_gamma[h] * rel), 0.0)
            qdec_ref[h] = jnp.exp(log_gamma[h] * (idx + 1.0))
            kdec_ref[h] = jnp.exp(log_gamma[h] * (tb - 1.0 - idx))

    @pl.when(pl.program_id(1) == 0)
    def _():
        state_ref[...] = jnp.zeros_like(state_ref)


def _retention_step(pos_ref, invf_ref, sgn_ref, q_ref, k_ref, v_ref, g_ref, o_ref,
                    state_ref, decay_ref, qdec_ref, kdec_ref, *, tb):
    heads = range(RET_HEADS)
    lanes = lambda h: slice(h * HEAD_DIM, (h + 1) * HEAD_DIM)
    log_gamma = _retention_log_gamma()

    pos = pos_ref[0].astype(F32)
    pos_rows = jnp.broadcast_to(pos, (V7X_LANES, tb)).T
    ang = pos_rows * invf_ref[...]
    cos2 = jnp.cos(ang)
    sin2 = jnp.sin(ang) * sgn_ref[...]
    scale = HEAD_DIM ** -0.5

    rot = lambda x: x * cos2 + pltpu.roll(x, HEAD_DIM // 2, 1) * sin2
    qr = [rot(q_ref[:, lanes(h)]) for h in heads]
    kr = [rot(k_ref[:, lanes(h)]) * scale for h in heads]
    v = [v_ref[:, lanes(h)] for h in heads]
    state = [state_ref[h] for h in heads]
    hb = tb // 2
    lo, hi = slice(0, hb), slice(hb, tb)
    qd = [qdec_ref[h, lo, :] for h in heads]
    kd = [kdec_ref[h, hi, :] for h in heads]
    dm = [decay_ref[h, lo, lo] for h in heads]
    gamma_hb = [math.exp(log_gamma[h] * hb) for h in heads]
    s_a = [_mm_nt(qr[h][lo], kr[h][lo]) for h in heads]
    s_b = [_mm_nt(qr[h][hi], kr[h][hi]) for h in heads]
    kv_a = [_mm_tn(kr[h][lo] * kd[h], v[h][lo]) for h in heads]
    kv_b = [_mm_tn(kr[h][hi] * kd[h], v[h][hi]) for h in heads]
    inter_a = [_mm(qr[h][lo] * qd[h], state[h]) for h in heads]
    mid = [gamma_hb[h] * state[h] + kv_a[h] for h in heads]
    intra_a = [_mm(s_a[h] * dm[h], v[h][lo]) for h in heads]
    intra_b = [_mm(s_b[h] * dm[h], v[h][hi]) for h in heads]
    inter_b = [_mm(qr[h][hi] * qd[h], mid[h]) for h in heads]
    for h in heads:
        state_ref[h] = gamma_hb[h] * mid[h] + kv_b[h]
        o = jnp.concatenate([intra_a[h] + inter_a[h], intra_b[h] + inter_b[h]], axis=0)
        mu = jnp.mean(o, axis=-1, keepdims=True)
        oc = o - mu
        var = jnp.mean(oc * oc, axis=-1, keepdims=True)
        o_ref[:, lanes(h)] = (oc * lax.rsqrt(var + EPS) * _silu(g_ref[:, lanes(h)])).astype(BF16)


def _softplus(x):
    return jnp.maximum(x, 0.0) + jnp.log1p(jnp.exp(-jnp.abs(x)))


def _unit_lower_inverses(ms, eye, blk16, blk32):
    d = [jnp.where(blk16, m, 0.0) for m in ms]
    d2 = [_mm(a, a) for a in d]
    d4 = [_mm(a, a) for a in d2]
    d8 = [_mm(a, a) for a in d4]
    p = [_mm(eye - a, eye + b) for a, b in zip(d, d2)]
    p = [_mm(a, eye + b) for a, b in zip(p, d4)]
    p = [_mm(a, eye + b) for a, b in zip(p, d8)]
    off16 = blk32 & jnp.logical_not(blk16)
    lp = [_mm(jnp.where(off16, m, 0.0), a) for m, a in zip(ms, p)]
    p = [a - _mm(a, b) for a, b in zip(p, lp)]
    lp = [_mm(jnp.where(blk32, 0.0, m), a) for m, a in zip(ms, p)]
    p = [a - _mm(a, b) for a, b in zip(p, lp)]
    return p


def _gdn_setup(pad_ref, state_ref):
    @pl.when(pl.program_id(1) == 0)
    def _():
        state_ref[...] = jnp.zeros_like(state_ref)
        pad_ref[:, 0:V7X_SUBLANES, :] = jnp.zeros((3, V7X_SUBLANES, GDN_WIDTH), F32)


def _gdn_step(q_ref, k_ref, v_ref, z_ref, s_ref, cw_ref, alog_ref, dtb_ref, nw_ref,
              o_ref, pad_ref, conv_ref, state_ref, *, tb):
    c_len = GDN_CHUNK
    nc = tb // c_len
    halo = V7X_SUBLANES

    groups = [(0, halo)] + [(lo, min(CONV_ROWS, tb - lo)) for lo in range(halo, tb, CONV_ROWS)]
    for a, ref in enumerate((q_ref, k_ref, v_ref)):
        w = cw_ref[:, a * GDN_WIDTH:(a + 1) * GDN_WIDTH]
        taps = [w[CONV_K - 1 - back:CONV_K - back, :] for back in range(CONV_K)]
        pad_ref[a, halo:2 * halo, :] = ref[0:halo, :]
        for lo, n in groups:
            acc = ref[lo:lo + n, :] * taps[0]
            for back in range(1, CONV_K):
                prev = pad_ref[a, halo - back:2 * halo - back, :] if lo == 0 else ref[lo - back:lo - back + n, :]
                acc = acc + prev * taps[back]
            conv_ref[a, lo:lo + n, :] = _silu(acc)
        pad_ref[a, 0:halo, :] = ref[tb - halo:tb, :]

    small = s_ref[...]
    lane = lax.broadcasted_iota(jnp.int32, (tb, SMALL_COLS), 1)
    g_all = jnp.where(lane < GDN_HEADS, -jnp.exp(alog_ref[...]) * _softplus(small + dtb_ref[...]), 0.0)
    beta_all = jax.nn.sigmoid(small)

    ri = lax.broadcasted_iota(jnp.int32, (tb, tb), 0)
    ci = lax.broadcasted_iota(jnp.int32, (tb, tb), 1)
    chunk_tri = ((ri >= ci) & ((ri >> 6) == (ci >> 6))).astype(F32)
    gc = jnp.dot(chunk_tri, g_all, precision=lax.Precision.HIGHEST, preferred_element_type=F32)
    gct = gc.T

    r64 = lax.broadcasted_iota(jnp.int32, (c_len, c_len), 0)
    c64 = lax.broadcasted_iota(jnp.int32, (c_len, c_len), 1)
    causal = r64 >= c64
    strict = r64 > c64
    eye = (r64 == c64).astype(F32)
    blk16 = (r64 >> 4) == (c64 >> 4)
    blk32 = (r64 >> 5) == (c64 >> 5)
    scale = HEAD_DIM ** -0.5

    heads = range(GDN_HEADS)
    chains = [(h, c) for h in heads for c in range(nc)]
    rows = lambda c: slice(c * c_len, (c + 1) * c_len)
    lanes = lambda h: slice(h * HEAD_DIM, (h + 1) * HEAD_DIM)

    qn, kn, kb, vb, kbe, qe, g_b = [], [], [], [], [], [], []
    for h in heads:
        qh, kh, vh = conv_ref[0, :, lanes(h)], conv_ref[1, :, lanes(h)], conv_ref[2, :, lanes(h)]
        g_h = jnp.broadcast_to(gc[:, h:h + 1], (tb, HEAD_DIM))
        b_h = jnp.broadcast_to(beta_all[:, GDN_HEADS + h:GDN_HEADS + h + 1], (tb, HEAD_DIM))
        e_h = jnp.exp(g_h)
        q_h = qh * lax.rsqrt(jnp.sum(qh * qh, axis=-1, keepdims=True) + EPS) * scale
        k_h = kh * lax.rsqrt(jnp.sum(kh * kh, axis=-1, keepdims=True) + EPS)
        kb_h = k_h * b_h
        qn.append(q_h), kn.append(k_h), kb.append(kb_h), vb.append(vh * b_h)
        kbe.append(kb_h * e_h), qe.append(q_h * e_h), g_b.append(g_h)

    decay = [jnp.exp(jnp.where(causal, g_b[h][rows(c), :c_len] - gct[h:h + 1, rows(c)], -jnp.inf))
             for h, c in chains]
    kq = [_mm_nt(jnp.concatenate([kb[h][rows(c)], qn[h][rows(c)]], axis=0), kn[h][rows(c)])
          for h, c in chains]
    ms = [jnp.where(strict, a[:c_len] * dc, 0.0) for a, dc in zip(kq, decay)]
    attn = [a[c_len:] * dc for a, dc in zip(kq, decay)]
    t_inv = _unit_lower_inverses(ms, eye, blk16, blk32)
    rhs = [jnp.concatenate([vb[h][rows(c)], kbe[h][rows(c)]], axis=1) for h, c in chains]
    sol = [r + _mm(t - eye, r) for t, r in zip(t_inv, rhs)]

    state = [state_ref[h] for h in heads]
    outs = [[] for _ in heads]
    for c in range(nc):
        last = (c + 1) * c_len - 1
        g_last = [g_b[h][last:last + 1, :] for h in heads]
        k_dec = [kn[h][rows(c)] * jnp.exp(g_last[h] - g_b[h][rows(c)]) for h in heads]
        w_s = [_mm(sol[h * nc + c][:, HEAD_DIM:], state[h]) for h in heads]
        q_s = [_mm(qe[h][rows(c)], state[h]) for h in heads]
        v_new = [sol[h * nc + c][:, :HEAD_DIM] - w_s[h] for h in heads]
        a_v = [_mm(attn[h * nc + c], v_new[h]) for h in heads]
        k_v = [_mm_tn(k_dec[h], v_new[h]) for h in heads]
        state = [state[h] * jnp.exp(g_last[h]) + k_v[h] for h in heads]
        for h in heads:
            outs[h].append(q_s[h] + a_v[h])

    for h in heads:
        state_ref[h] = state[h]
        o = jnp.concatenate(outs[h], axis=0)
        o = o * lax.rsqrt(jnp.mean(o * o, axis=-1, keepdims=True) + EPS) * nw_ref[...]
        o_ref[:, lanes(h)] = (o * _silu(z_ref[:, lanes(h)])).astype(BF16)


def _mixer_kernel(pos_ref, invf_ref, sgn_ref, rq_ref, rk_ref, rv_ref, rg_ref,
                  gq_ref, gk_ref, gv_ref, gz_ref, s_ref, cw_ref, alog_ref, dtb_ref, nw_ref,
                  ro_ref, go_ref, r_state, r_decay, r_qdec, r_kdec, g_pad, g_conv, g_state, *, tb):
    _gdn_setup(g_pad, g_state)
    _retention_setup(r_state, r_decay, r_qdec, r_kdec, tb=tb)
    _gdn_step(gq_ref, gk_ref, gv_ref, gz_ref, s_ref, cw_ref, alog_ref, dtb_ref, nw_ref,
              go_ref, g_pad, g_conv, g_state, tb=tb)
    _retention_step(pos_ref, invf_ref, sgn_ref, rq_ref, rk_ref, rv_ref, rg_ref, ro_ref,
                    r_state, r_decay, r_qdec, r_kdec, tb=tb)


def _mixer(proj, small, pos3, invf2, sgn2, conv_w, alog_row, dtb_row, norm_w, batch, seq):
    assert RET_TB == GDN_TB
    tb = GDN_TB
    nt = seq // tb
    m = batch * seq
    col = lambda c: pl.BlockSpec((tb, RET_WIDTH), lambda b, t, c=c: (b * nt + t, c))
    row = lambda n: pl.BlockSpec((1, n), lambda b, t: (0, 0))
    out = pl.BlockSpec((tb, GDN_WIDTH), lambda b, t: (b * nt + t, 0))
    blocks = (8 * tb * GDN_WIDTH * 4 + tb * SMALL_COLS * 4 + CONV_K * 3 * GDN_WIDTH * 4
              + 2 * tb * GDN_WIDTH * 2 + tb * 4 + 5 * V7X_LANES * 4)
    scratch = (RET_HEADS * (HEAD_DIM * HEAD_DIM + tb * tb + 2 * tb * HEAD_DIM) * 4
               + 3 * 2 * V7X_SUBLANES * GDN_WIDTH * 4 + 3 * tb * GDN_WIDTH * 4
               + GDN_HEADS * HEAD_DIM * HEAD_DIM * 4)
    return pl.pallas_call(
        functools.partial(_mixer_kernel, tb=tb),
        grid=(batch, nt),
        in_specs=[
            pl.BlockSpec((1, 1, tb), lambda b, t: (b * nt + t, 0, 0)), row(V7X_LANES), row(V7X_LANES),
            col(0), col(1), col(2), col(3), col(4), col(5), col(6), col(7),
            pl.BlockSpec((tb, SMALL_COLS), lambda b, t: (b * nt + t, 0)),
            pl.BlockSpec((CONV_K, 3 * GDN_WIDTH), lambda b, t: (0, 0)),
            row(SMALL_COLS), row(SMALL_COLS), row(HEAD_DIM),
        ],
        out_specs=[out, out],
        out_shape=[jax.ShapeDtypeStruct((m, RET_WIDTH), BF16), jax.ShapeDtypeStruct((m, GDN_WIDTH), BF16)],
        scratch_shapes=[
            pltpu.VMEM((RET_HEADS, HEAD_DIM, HEAD_DIM), F32),
            pltpu.VMEM((RET_HEADS, tb, tb), F32),
            pltpu.VMEM((RET_HEADS, tb, HEAD_DIM), F32),
            pltpu.VMEM((RET_HEADS, tb, HEAD_DIM), F32),
            pltpu.VMEM((3, 2 * V7X_SUBLANES, GDN_WIDTH), F32),
            pltpu.VMEM((3, tb, GDN_WIDTH), F32),
            pltpu.VMEM((GDN_HEADS, HEAD_DIM, HEAD_DIM), F32),
        ],
        compiler_params=pltpu.CompilerParams(
            dimension_semantics=("arbitrary", "arbitrary"),
            vmem_limit_bytes=_vmem_limit(blocks, scratch, 16 * tb * tb * 4 + 20 * tb * GDN_WIDTH * 4)),
        name="mixer",
    )(pos3, invf2, sgn2, *([proj] * 8), small, conv_w, alog_row, dtb_row, norm_w)


def _outproj_kernel(x_ref, ro_ref, go_ref, wr_ref, wg_ref, o_ref):
    o_ref[...] = (x_ref[...]
                  + jnp.dot(ro_ref[...], wr_ref[...], preferred_element_type=F32)
                  + jnp.dot(go_ref[...], wg_ref[...], preferred_element_type=F32))


def _outproj(x, ro, go, w_out):
    m, d = x.shape
    tm = OUTPROJ_TM
    blocks = 2 * tm * d * 4 + tm * (RET_WIDTH + GDN_WIDTH) * 2 + (RET_WIDTH + GDN_WIDTH) * d * 2
    return pl.pallas_call(
        _outproj_kernel,
        grid=(m // tm,),
        in_specs=[
            pl.BlockSpec((tm, d), lambda i: (i, 0)),
            pl.BlockSpec((tm, RET_WIDTH), lambda i: (i, 0)),
            pl.BlockSpec((tm, GDN_WIDTH), lambda i: (i, 0)),
            pl.BlockSpec((RET_WIDTH, d), lambda i: (0, 0)),
            pl.BlockSpec((GDN_WIDTH, d), lambda i: (RET_WIDTH // GDN_WIDTH, 0)),
        ],
        out_specs=pl.BlockSpec((tm, d), lambda i: (i, 0)),
        out_shape=jax.ShapeDtypeStruct((m, d), F32),
        compiler_params=pltpu.CompilerParams(
            dimension_semantics=("parallel",),
            vmem_limit_bytes=_vmem_limit(blocks, 0, 2 * tm * d * 4)),
        name="outproj",
    )(x, ro, go, w_out, w_out)


def kernel(x, positions, norm_ffn1_w, ffn1_w_gate, ffn1_w_up, ffn1_w_down, norm_mix_w, w_in, conv_w,
           gdn_a_log, gdn_dt_bias, gdn_norm_w, w_out, norm_ffn2_w, ffn2_w_gate, ffn2_w_up, ffn2_w_down,
           norm_final_w):
    batch, seq, d = x.shape
    depth = norm_ffn1_w.shape[0]
    m = batch * seq
    n_main = 4 * RET_WIDTH + 4 * GDN_WIDTH
    row = lambda v: v.reshape(1, -1).astype(F32)

    half = HEAD_DIM // 2
    inv_freq = ROPE_THETA ** (-jnp.arange(half, dtype=F32) / half)
    invf2 = jnp.concatenate([inv_freq, inv_freq]).reshape(1, HEAD_DIM)
    sgn2 = jnp.concatenate([-jnp.ones((half,), F32), jnp.ones((half,), F32)]).reshape(1, HEAD_DIM)
    pos3 = positions.reshape(m // RET_TB, 1, RET_TB)
    pad_heads = lambda v: jnp.pad(v.astype(F32), (0, SMALL_COLS - GDN_HEADS)).reshape(1, SMALL_COLS)

    xf = x.reshape(m, d)
    for l in range(depth):
        last = l == depth - 1
        w_in_t = jnp.swapaxes(w_in[l], 0, 1)
        later = ((w_in_t, n_main), (w_out[l], w_out.shape[1]), (ffn2_w_gate[l], d), (ffn2_w_up[l], d),
                 (ffn2_w_down[l], ffn2_w_down.shape[1]))
        head, wg1, wu1, wd1 = _ffn_head(xf, row(norm_ffn1_w[l]), ffn1_w_gate[l], ffn1_w_up[l], ffn1_w_down[l])
        xf, (w_in_b, w_out_b, wg2, wu2, wd2) = _ffn(
            xf, row(norm_ffn1_w[l]), wg1, wu1, wd1, row(norm_final_w), later, final_norm=False, done=(head, 1))
        w_small = jnp.pad(w_in_t[n_main:], ((0, SMALL_COLS - 2 * GDN_HEADS), (0, 0))).astype(BF16)
        proj, small = _inproj(xf, row(norm_mix_w[l]), w_in_b, w_small, n_main)
        ro, go = _mixer(proj, small, pos3, invf2, sgn2, conv_w[l].astype(F32), pad_heads(gdn_a_log[l]),
                        pad_heads(gdn_dt_bias[l]), row(gdn_norm_w[l]), batch, seq)
        xf = _outproj(xf, ro, go, w_out_b)
        xf, _ = _ffn(xf, row(norm_ffn2_w[l]), wg2, wu2, wd2, row(norm_final_w), final_norm=last)
    return xf.reshape(batch, seq, d)
```
